```python
import jax, jax.numpy as jnp
from jax import lax
import numpy as np

D_MODEL = 2048
BATCH = 2
SEQ = 4096
DEPTH = 4

N_MIXERS = 2
N_A_LAYERS = (DEPTH + N_MIXERS - 1) // N_MIXERS
N_B_LAYERS = DEPTH // N_MIXERS
HGRN_EXPAND = 128
HGRN_HEADS = D_MODEL // HGRN_EXPAND
HGRN_KEY_DIM = HGRN_HEADS * HGRN_EXPAND
HGRN_HEAD_V = D_MODEL // HGRN_HEADS
CHUNK = 64
CONV_WIDTH = 3
FFN_DIM = 5632
EPS = 1e-6

kernel_name = "bidir_hgrn2_shortconv_convffn_hybrid"


def rmsnorm(x, w):
    x32 = x.astype(jnp.float32)
    y = x32 * lax.rsqrt(jnp.mean(x32 * x32, axis=-1, keepdims=True) + EPS)
    return (y * w.astype(jnp.float32)).astype(x.dtype)


def dwconv3(z, w):
    L = z.shape[1]
    zp = jnp.pad(z, ((0, 0), (1, 1), (0, 0)))
    return w[0] * zp[:, :L] + w[1] * zp[:, 1:L + 1] + w[2] * zp[:, 2:]


def chunked_gated_recurrence(q, k, log_f, v):
    N, H, L, K = q.shape
    V = v.shape[-1]
    n_chunks = L // CHUNK

    def to_chunks(t):
        return t.reshape(N, H, n_chunks, CHUNK, t.shape[-1]).transpose(2, 0, 1, 3, 4)

    causal_in_chunk = jnp.tril(jnp.ones((CHUNK, CHUNK), dtype=bool))[:, :, None]

    def step(S, inp):
        qi, ki, gi, vi = inp
        b = jnp.cumsum(gi, axis=2)
        inter = jnp.einsum('nhtk,nhkv->nhtv', qi * jnp.exp(b), S)
        diff = b[:, :, :, None, :] - b[:, :, None, :, :]
        decay = jnp.exp(jnp.where(causal_in_chunk, diff, -jnp.inf))
        scores = jnp.einsum('nhtk,nhsk,nhtsk->nhts', qi, ki, decay)
        intra = jnp.einsum('nhts,nhsv->nhtv', scores, vi)
        b_last = b[:, :, -1]
        k_dec = ki * jnp.exp(b_last[:, :, None, :] - b)
        S_new = jnp.exp(b_last)[..., None] * S + jnp.einsum('nhck,nhcv->nhkv', k_dec, vi)
        return S_new, inter + intra

    S0 = jnp.zeros((N, H, K, V), jnp.float32)
    _, o = lax.scan(step, S0, (to_chunks(q), to_chunks(k), to_chunks(log_f), to_chunks(v)))
    return o.transpose(1, 2, 0, 3, 4).reshape(N, H, L, V)


def hgrn2_mixer(h, w_in, w_out, g_norm_w, lb_fwd, lb_bwd):
    B_, L, _ = h.shape
    K, D = HGRN_KEY_DIM, D_MODEL
    u = h @ w_in
    q, f_f, f_b, v, g = jnp.split(u, [K, 2 * K, 3 * K, 3 * K + D], axis=-1)
    q = jax.nn.silu(q)

    def gates(f_raw, lb):
        f = lb + (1.0 - lb) * jax.nn.sigmoid(f_raw.astype(jnp.float32))
        return 1.0 - f, jnp.log(f)

    k_f, lf_f = gates(f_f, lb_fwd)
    k_b, lf_b = gates(f_b, lb_bwd)

    def heads(t, d):
        return t.reshape(t.shape[0], L, HGRN_HEADS, d).transpose(0, 2, 1, 3).astype(jnp.float32)

    rev = lambda t: t[:, ::-1]
    qs = heads(jnp.concatenate([q, rev(q)], axis=0), HGRN_EXPAND)
    ks = heads(jnp.concatenate([k_f, rev(k_b)], axis=0), HGRN_EXPAND)
    ls = heads(jnp.concatenate([lf_f, rev(lf_b)], axis=0), HGRN_EXPAND)
    vs = heads(jnp.concatenate([v, rev(v)], axis=0), HGRN_HEAD_V)
    o = chunked_gated_recurrence(qs, ks, ls, vs)
    o = o[:B_] + o[B_:, :, ::-1]
    o = o.transpose(0, 2, 1, 3)
    gh = g.reshape(B_, L, HGRN_HEADS, HGRN_HEAD_V)
    o = rmsnorm(o, g_norm_w) * jax.nn.silu(gh.astype(jnp.float32))
    return o.reshape(B_, L, D).astype(h.dtype) @ w_out


def short_conv_mixer(h, w_in, w_conv, w_out):
    u = h @ w_in
    gate_b, gate_c, x_in = jnp.split(u, 3, axis=-1)
    z = dwconv3(gate_c * x_in, w_conv)
    return (gate_b * z) @ w_out


def conv_ffn(h, w_in, w_conv, w_out):
    up = dwconv3(h @ w_in, w_conv)
    gate, val = jnp.split(up, 2, axis=-1)
    return (jax.nn.silu(gate) * val) @ w_out


def setup_inputs(seed: int = 0) -> dict:
    key = jax.random.key(seed)
    ks = jax.random.split(key, 16)
    D, K, F = D_MODEL, HGRN_KEY_DIM, FFN_DIM
    nrm = lambda k, shape, scale: jax.random.normal(k, shape, jnp.float32) * scale
    return {
        "x": nrm(ks[0], (BATCH, SEQ, D), 1.0),
        "hgrn_w_in": nrm(ks[1], (N_A_LAYERS, D, 3 * K + 2 * D), D ** -0.5),
        "hgrn_w_out": nrm(ks[2], (N_A_LAYERS, D, D), D ** -0.5),
        "hgrn_gnorm": 1.0 + nrm(ks[3], (N_A_LAYERS, HGRN_HEAD_V), 0.02),
        "hgrn_lower_bounds": nrm(ks[4], (2, N_A_LAYERS, K), 1.0),
        "sconv_w_in": nrm(ks[5], (N_B_LAYERS, D, 3 * D), D ** -0.5),
        "sconv_w_conv": nrm(ks[6], (N_B_LAYERS, CONV_WIDTH, D), CONV_WIDTH ** -0.5),
        "sconv_w_out": nrm(ks[7], (N_B_LAYERS, D, D), D ** -0.5),
        "ffn_w_in": nrm(ks[8], (DEPTH, D, 2 * F), D ** -0.5),
        "ffn_w_conv": nrm(ks[9], (DEPTH, CONV_WIDTH, 2 * F), CONV_WIDTH ** -0.5),
        "ffn_w_out": nrm(ks[10], (DEPTH, F, D), F ** -0.5),
        "norm_pre_mix": 1.0 + nrm(ks[11], (DEPTH, D), 0.02),
        "norm_post_mix": 1.0 + nrm(ks[12], (DEPTH, D), 0.02),
        "norm_pre_ffn": 1.0 + nrm(ks[13], (DEPTH, D), 0.02),
        "norm_post_ffn": 1.0 + nrm(ks[14], (DEPTH, D), 0.02),
    }


def reference(x, hgrn_w_in, hgrn_w_out, hgrn_gnorm, hgrn_lower_bounds,
              sconv_w_in, sconv_w_conv, sconv_w_out,
              ffn_w_in, ffn_w_conv, ffn_w_out,
              norm_pre_mix, norm_post_mix, norm_pre_ffn, norm_post_ffn):
    lb = jnp.cumsum(jax.nn.softmax(hgrn_lower_bounds.astype(jnp.float32), axis=1), axis=1)
    lb = lb - lb[:, :1]
    for i in range(DEPTH):
        j = i // N_MIXERS
        h = rmsnorm(x, norm_pre_mix[i])
        if i % N_MIXERS == 0:
            m = hgrn2_mixer(h, hgrn_w_in[j], hgrn_w_out[j], hgrn_gnorm[j], lb[0, j], lb[1, j])
        else:
            m = short_conv_mixer(h, sconv_w_in[j], sconv_w_conv[j], sconv_w_out[j])
        x = x + rmsnorm(m, norm_post_mix[i])
        h = rmsnorm(x, norm_pre_ffn[i])
        x = x + rmsnorm(conv_ffn(h, ffn_w_in[i], ffn_w_conv[i], ffn_w_out[i]), norm_post_ffn[i])
    return x
```

```python
import functools

import jax
import jax.numpy as jnp
from jax import lax
from jax.experimental import pallas as pl
from jax.experimental.pallas import tpu as pltpu

D_MODEL = 2048
HEADS = 16
HEAD_DIM = 128
EPS = 1e-6

V7X_LANES = 128
V7X_VMEM_LIMIT_BYTES = 56 * 1024 * 1024

CHUNK = 64
SUB = 16
PAD = SUB


def _cparams(*sem):
    return pltpu.CompilerParams(dimension_semantics=sem, vmem_limit_bytes=V7X_VMEM_LIMIT_BYTES)


def _silu(x):
    return x * (1.0 / (1.0 + jnp.exp(-x)))


def _rms(x, w):
    return x * lax.rsqrt(jnp.mean(x * x, axis=-1, keepdims=True) + EPS) * w


def _norm_kernel(x_ref, w_ref, h_ref):
    h_ref[...] = _rms(x_ref[...], w_ref[...]).astype(h_ref.dtype)


def _norm(x, w, bm=512):
    t, d = x.shape
    return pl.pallas_call(
        _norm_kernel,
        grid=(t // bm,),
        in_specs=[pl.BlockSpec((bm, d), lambda i: (i, 0)), pl.BlockSpec((1, d), lambda i: (0, 0))],
        out_specs=pl.BlockSpec((bm, d), lambda i: (i, 0)),
        out_shape=jax.ShapeDtypeStruct((t, d), jnp.bfloat16),
        compiler_params=_cparams("parallel"),
        name="rmsnorm",
    )(x, w.reshape(1, d))


def _mm_kernel(a_ref, w_ref, o_ref, *, act):
    acc = jnp.dot(a_ref[...], w_ref[...], preferred_element_type=jnp.float32)
    if act == "silu":
        acc = _silu(acc)
    o_ref[...] = acc.astype(o_ref.dtype)


def _mm(a, w, col0, n, *, act=None, out_dtype=jnp.bfloat16, bm=1024, bn=1024, name="mm"):
    t, k = a.shape
    cb = col0 // bn
    return pl.pallas_call(
        functools.partial(_mm_kernel, act=act),
        grid=(n // bn, t // bm),
        in_specs=[pl.BlockSpec((bm, k), lambda j, i: (i, 0)),
                  pl.BlockSpec((k, bn), lambda j, i: (0, cb + j))],
        out_specs=pl.BlockSpec((bm, bn), lambda j, i: (i, j)),
        out_shape=jax.ShapeDtypeStruct((t, n), out_dtype),
        compiler_params=_cparams("parallel", "parallel"),
        name=name,
    )(a, w)


def _gate_kernel(a_ref, w_ref, lbraw_ref, o_ref, *, layer):
    x = jnp.dot(a_ref[...], w_ref[...], preferred_element_type=jnp.float32)
    raw = lbraw_ref[...]
    e = jnp.exp(raw - jnp.max(raw, axis=0, keepdims=True))
    p = e / jnp.sum(e, axis=0, keepdims=True)
    cs = p[0:1]
    first = cs
    for r in range(1, layer + 1):
        cs = cs + p[r:r + 1]
    lb = cs - first
    f = lb + (1.0 - lb) * (1.0 / (1.0 + jnp.exp(-x)))
    o_ref[...] = jnp.log(f)


def _mm_gate(a, w, col0, n, lbraw, layer, *, bm=1024, bn=1024, name="mm_gate"):
    t, k = a.shape
    cb = col0 // bn
    nl = lbraw.shape[0]
    return pl.pallas_call(
        functools.partial(_gate_kernel, layer=layer),
        grid=(n // bn, t // bm),
        in_specs=[pl.BlockSpec((bm, k), lambda j, i: (i, 0)),
                  pl.BlockSpec((k, bn), lambda j, i: (0, cb + j)),
                  pl.BlockSpec((nl, bn), lambda j, i: (0, j))],
        out_specs=pl.BlockSpec((bm, bn), lambda j, i: (i, j)),
        out_shape=jax.ShapeDtypeStruct((t, n), jnp.float32),
        compiler_params=_cparams("parallel", "parallel"),
        name=name,
    )(a, w, lbraw)


def _mm_mul_kernel(a_ref, w1_ref, w2_ref, o_ref):
    a = a_ref[...]
    c = jnp.dot(a, w1_ref[...], preferred_element_type=jnp.float32)
    x = jnp.dot(a, w2_ref[...], preferred_element_type=jnp.float32)
    o_ref[...] = (c * x).astype(o_ref.dtype)


def _mm_mul(a, w, col1, col2, n, *, bm=1024, bn=512, name="mm_mul"):
    t, k = a.shape
    c1, c2 = col1 // bn, col2 // bn
    return pl.pallas_call(
        _mm_mul_kernel,
        grid=(n // bn, t // bm),
        in_specs=[pl.BlockSpec((bm, k), lambda j, i: (i, 0)),
                  pl.BlockSpec((k, bn), lambda j, i: (0, c1 + j)),
                  pl.BlockSpec((k, bn), lambda j, i: (0, c2 + j))],
        out_specs=pl.BlockSpec((bm, bn), lambda j, i: (i, j)),
        out_shape=jax.ShapeDtypeStruct((t, n), jnp.bfloat16),
        compiler_params=_cparams("parallel", "parallel"),
        name=name,
    )(a, w, w)


def _resnorm_kernel(x_ref, m_ref, wpost_ref, wnext_ref, xo_ref, h_ref):
    xn = x_ref[...] + _rms(m_ref[...], wpost_ref[...])
    xo_ref[...] = xn
    h_ref[...] = _rms(xn, wnext_ref[...]).astype(h_ref.dtype)


def _resnorm(x, m, wpost, wnext, bm=512):
    t, d = x.shape
    row = pl.BlockSpec((bm, d), lambda i: (i, 0))
    vec = pl.BlockSpec((1, d), lambda i: (0, 0))
    return pl.pallas_call(
        _resnorm_kernel,
        grid=(t // bm,),
        in_specs=[row, row, vec, vec],
        out_specs=[row, row],
        out_shape=[jax.ShapeDtypeStruct((t, d), jnp.float32), jax.ShapeDtypeStruct((t, d), jnp.bfloat16)],
        compiler_params=_cparams("parallel"),
        name="resnorm",
    )(x, m, wpost.reshape(1, d), wnext.reshape(1, d))


def _conv3(x, w_ref):
    n = x.shape[0]
    row = lax.broadcasted_iota(jnp.int32, x.shape, 0)
    prev = jnp.where(row == 0, 0.0, pltpu.roll(x, 1, 0))
    nxt = jnp.where(row == n - 1, 0.0, pltpu.roll(x, n - 1, 0))
    return w_ref[0:1, :] * prev + w_ref[1:2, :] * x + w_ref[2:3, :] * nxt


def _sconv_mid_kernel(b_ref, p_ref, w_ref, o_ref):
    z = _conv3(p_ref[0].astype(jnp.float32), w_ref)
    o_ref[0] = (b_ref[0].astype(jnp.float32) * z).astype(o_ref.dtype)


def _sconv_mid(gate_b, p, w_conv, bc=256):
    b, l, c = p.shape
    blk = pl.BlockSpec((1, l, bc), lambda i, j: (i, 0, j))
    return pl.pallas_call(
        _sconv_mid_kernel,
        grid=(b, c // bc),
        in_specs=[blk, blk, pl.BlockSpec((3, bc), lambda i, j: (0, j))],
        out_specs=blk,
        out_shape=jax.ShapeDtypeStruct((b, l, c), jnp.bfloat16),
        compiler_params=_cparams("parallel", "parallel"),
        name="sconv_mid",
    )(gate_b, p, w_conv)


def _ffn_mid_kernel(g_ref, v_ref, wg_ref, wv_ref, o_ref):
    g = _conv3(g_ref[0].astype(jnp.float32), wg_ref)
    v = _conv3(v_ref[0].astype(jnp.float32), wv_ref)
    o_ref[0] = (_silu(g) * v).astype(o_ref.dtype)


def _ffn_mid(up, w_conv, f, bc=256):
    b, l, _ = up.shape
    nb = f // bc
    return pl.pallas_call(
        _ffn_mid_kernel,
        grid=(b, nb),
        in_specs=[pl.BlockSpec((1, l, bc), lambda i, j: (i, 0, j)),
                  pl.BlockSpec((1, l, bc), lambda i, j: (i, 0, nb + j)),
                  pl.BlockSpec((3, bc), lambda i, j: (0, j)),
                  pl.BlockSpec((3, bc), lambda i, j: (0, nb + j))],
        out_specs=pl.BlockSpec((1, l, bc), lambda i, j: (i, 0, j)),
        out_shape=jax.ShapeDtypeStruct((b, l, f), jnp.bfloat16),
        compiler_params=_cparams("parallel", "parallel"),
        name="ffn_mid",
    )(up, up, w_conv, w_conv)


def _cumsum_rows(x, reverse):
    n = x.shape[0]
    row = lax.broadcasted_iota(jnp.int32, x.shape, 0)
    off = 1
    while off < n:
        if reverse:
            x = x + jnp.where(row < n - off, pltpu.roll(x, n - off, 0), 0.0)
        else:
            x = x + jnp.where(row >= off, pltpu.roll(x, off, 0), 0.0)
        off *= 2
    return x


def _chunk_step(q, lf, v, st, kpad_ref, fpad_ref, band_id, reverse):
    c = CHUNK
    nblk = c // SUB
    row = lax.broadcasted_iota(jnp.int32, (c, 1), 0)
    b = _cumsum_rows(lf, reverse)
    f = jnp.exp(lf)
    k = 1.0 - f
    b_last = b[0:1] if reverse else b[c - 1:c]

    qe = (q * jnp.exp(b)).astype(jnp.bfloat16)
    inter = lax.dot_general(qe, st.astype(jnp.bfloat16), (((1,), (1,)), ((), ())),
                            preferred_element_type=jnp.float32)

    parts = []
    for i in range(nblk):
        lo, hi = i * SUB, (i + 1) * SUB
        if (not reverse and i == 0) or (reverse and i == nblk - 1):
            parts.append(jnp.zeros((SUB, c), jnp.float32))
            continue
        beta = b[hi:hi + 1] if reverse else b[lo - 1:lo]
        earlier = (row >= hi) if reverse else (row < lo)
        qi = (q[lo:hi] * jnp.exp(b[lo:hi] - beta)).astype(jnp.bfloat16)
        ki = (k * jnp.exp(jnp.where(earlier, beta - b, -jnp.inf))).astype(jnp.bfloat16)
        parts.append(lax.dot_general(qi, ki, (((1,), (1,)), ((), ())),
                                     preferred_element_type=jnp.float32))
    scores = jnp.concatenate(parts, axis=0)

    kpad_ref[PAD:PAD + c, :] = k
    fpad_ref[PAD:PAD + c, :] = f
    sgn = 1 if reverse else -1
    prod = q
    for d in range(SUB):
        if d > 0:
            s0 = PAD + sgn * (d - 1)
            prod = prod * fpad_ref[s0:s0 + c, :]
        s0 = PAD + sgn * d
        a_d = jnp.sum(prod * kpad_ref[s0:s0 + c, :], axis=-1, keepdims=True)
        scores = jnp.where(band_id == d, a_d, scores)

    intra = jnp.dot(scores.astype(jnp.bfloat16), v, preferred_element_type=jnp.float32)
    kdec = (k * jnp.exp(b_last - b)).astype(jnp.bfloat16)
    st_new = st * jnp.exp(b_last) + lax.dot_general(
        v, kdec, (((0,), (0,)), ((), ())), preferred_element_type=jnp.float32)
    return inter + intra, st_new


def _hgrn_kernel(q_ref, lff_ref, lfb_ref, v_ref, gs_ref, gw_ref, o_ref,
                 of_ref, ob_ref, sf_ref, sb_ref, kpf_ref, fpf_ref, kpb_ref, fpb_ref):
    l = q_ref.shape[1]
    nchunk = l // CHUNK
    sf_ref[...] = jnp.zeros_like(sf_ref)
    sb_ref[...] = jnp.zeros_like(sb_ref)
    for r in (kpf_ref, fpf_ref, kpb_ref, fpb_ref):
        r[...] = jnp.zeros_like(r)

    r_i = lax.broadcasted_iota(jnp.int32, (CHUNK, CHUNK), 0)
    c_i = lax.broadcasted_iota(jnp.int32, (CHUNK, CHUNK), 1)
    same = (r_i // SUB) == (c_i // SUB)
    band_f = jnp.where(same, r_i - c_i, -1)
    band_b = jnp.where(same, c_i - r_i, -1)

    def body(ci, carry):
        rf = pl.multiple_of(ci * CHUNK, CHUNK)
        rb = pl.multiple_of((nchunk - 1 - ci) * CHUNK, CHUNK)
        o, s = _chunk_step(q_ref[0, pl.ds(rf, CHUNK), :].astype(jnp.float32),
                           lff_ref[0, pl.ds(rf, CHUNK), :], v_ref[0, pl.ds(rf, CHUNK), :],
                           sf_ref[...], kpf_ref, fpf_ref, band_f, False)
        of_ref[pl.ds(rf, CHUNK), :] = o
        sf_ref[...] = s
        o, s = _chunk_step(q_ref[0, pl.ds(rb, CHUNK), :].astype(jnp.float32),
                           lfb_ref[0, pl.ds(rb, CHUNK), :], v_ref[0, pl.ds(rb, CHUNK), :],
                           sb_ref[...], kpb_ref, fpb_ref, band_b, True)
        ob_ref[pl.ds(rb, CHUNK), :] = o
        sb_ref[...] = s
        return carry

    lax.fori_loop(0, nchunk, body, 0)

    def fin(ci, carry):
        r0 = pl.multiple_of(ci * CHUNK, CHUNK)
        o = of_ref[pl.ds(r0, CHUNK), :] + ob_ref[pl.ds(r0, CHUNK), :]
        y = _rms(o, gw_ref[...]) * gs_ref[0, pl.ds(r0, CHUNK), :].astype(jnp.float32)
        o_ref[0, pl.ds(r0, CHUNK), :] = y.astype(o_ref.dtype)
        return carry

    lax.fori_loop(0, nchunk, fin, 0)


def _hgrn_recurrence(q, lff, lfb, v, gs, gw):
    b, l, d = q.shape
    hd = HEAD_DIM
    blk = pl.BlockSpec((1, l, hd), lambda i, j: (i, 0, j))
    f32 = jnp.float32
    return pl.pallas_call(
        _hgrn_kernel,
        grid=(b, d // hd),
        in_specs=[blk, blk, blk, blk, blk, pl.BlockSpec((1, hd), lambda i, j: (0, 0))],
        out_specs=blk,
        out_shape=jax.ShapeDtypeStruct((b, l, d), jnp.bfloat16),
        scratch_shapes=[pltpu.VMEM((l, hd), f32), pltpu.VMEM((l, hd), f32),
                        pltpu.VMEM((hd, hd), f32), pltpu.VMEM((hd, hd), f32),
                        pltpu.VMEM((CHUNK + 2 * PAD, hd), f32), pltpu.VMEM((CHUNK + 2 * PAD, hd), f32),
                        pltpu.VMEM((CHUNK + 2 * PAD, hd), f32), pltpu.VMEM((CHUNK + 2 * PAD, hd), f32)],
        compiler_params=_cparams("parallel", "parallel"),
        name="hgrn_recurrence",
    )(q, lff, lfb, v, gs, gw.reshape(1, hd))


def kernel(x, hgrn_w_in, hgrn_w_out, hgrn_gnorm, hgrn_lower_bounds, sconv_w_in, sconv_w_conv, sconv_w_out,
           ffn_w_in, ffn_w_conv, ffn_w_out, norm_pre_mix, norm_post_mix, norm_pre_ffn, norm_post_ffn):
    bsz, seq, d = x.shape
    t = bsz * seq
    depth = norm_pre_mix.shape[0]
    ffn_dim = ffn_w_out.shape[1]
    bf16 = jnp.bfloat16
    f32 = jnp.float32

    xf = x.reshape(t, d)
    h = _norm(xf, norm_pre_mix[0])
    for i in range(depth):
        j = i // 2
        if i % 2 == 0:
            w_in = hgrn_w_in[j].astype(bf16)
            q = _mm(h, w_in, 0, d, act="silu", name="hgrn_q")
            lff = _mm_gate(h, w_in, d, d, hgrn_lower_bounds[0], j, name="hgrn_gate_f")
            lfb = _mm_gate(h, w_in, 2 * d, d, hgrn_lower_bounds[1], j, name="hgrn_gate_b")
            v = _mm(h, w_in, 3 * d, d, name="hgrn_v")
            gs = _mm(h, w_in, 4 * d, d, act="silu", name="hgrn_g")
            sh = (bsz, seq, d)
            o = _hgrn_recurrence(q.reshape(sh), lff.reshape(sh), lfb.reshape(sh), v.reshape(sh),
                                 gs.reshape(sh), hgrn_gnorm[j])
            m = _mm(o.reshape(t, d), hgrn_w_out[j].astype(bf16), 0, d, out_dtype=f32, name="hgrn_out")
        else:
            w_in = sconv_w_in[j].astype(bf16)
            gate_b = _mm(h, w_in, 0, d, name="sconv_b")
            p = _mm_mul(h, w_in, d, 2 * d, d, name="sconv_cx")
            sh = (bsz, seq, d)
            y = _sconv_mid(gate_b.reshape(sh), p.reshape(sh), sconv_w_conv[j])
            m = _mm(y.reshape(t, d), sconv_w_out[j].astype(bf16), 0, d, out_dtype=f32, name="sconv_out")
        xf, h = _resnorm(xf, m, norm_post_mix[i], norm_pre_ffn[i])

        up = _mm(h, ffn_w_in[i].astype(bf16), 0, 2 * ffn_dim, bn=1024, name="ffn_in")
        a = _ffn_mid(up.reshape(bsz, seq, 2 * ffn_dim), ffn_w_conv[i], ffn_dim)
        m = _mm(a.reshape(t, ffn_dim), ffn_w_out[i].astype(bf16), 0, d, out_dtype=f32, bm=512, name="ffn_out")
        w_next = norm_pre_mix[i + 1] if i + 1 < depth else norm_pre_mix[0]
        xf, h = _resnorm(xf, m, norm_post_ffn[i], w_next)
    return xf.reshape(bsz, seq, d)
```

```python
import functools

import jax
import jax.numpy as jnp
from jax import lax
from jax.experimental import pallas as pl
from jax.experimental.pallas import tpu as pltpu

D_MODEL = 2048
HEADS = 16
HEAD_DIM = 128
EPS = 1e-6

V7X_LANES = 128
V7X_VMEM_LIMIT_BYTES = 56 * 1024 * 1024

CHUNK = 64
BAND = 8
UNROLL = 8
FIN_ROWS = 512
FFN_OUT_BK = 1408


def _cparams(*sem):
    return pltpu.CompilerParams(dimension_semantics=sem, vmem_limit_bytes=V7X_VMEM_LIMIT_BYTES)


def _silu(x):
    return x * (1.0 / (1.0 + jnp.exp(-x)))


def _rms(x, w):
    return x * lax.rsqrt(jnp.mean(x * x, axis=-1, keepdims=True) + EPS) * w


def _norm_kernel(x_ref, w_ref, h_ref):
    h_ref[...] = _rms(x_ref[...], w_ref[...]).astype(h_ref.dtype)


def _norm(x, w, bm=512):
    t, d = x.shape
    return pl.pallas_call(
        _norm_kernel,
        grid=(t // bm,),
        in_specs=[pl.BlockSpec((bm, d), lambda i: (i, 0)), pl.BlockSpec((1, d), lambda i: (0, 0))],
        out_specs=pl.BlockSpec((bm, d), lambda i: (i, 0)),
        out_shape=jax.ShapeDtypeStruct((t, d), jnp.bfloat16),
        compiler_params=_cparams("parallel"),
        name="rmsnorm",
    )(x, w.reshape(1, d))


def _cast_weights(w_ref, wb_ref):
    @pl.when(pl.program_id(1) == 0)
    def _():
        wb_ref[...] = w_ref[...].astype(wb_ref.dtype)


def _mm_kernel(a_ref, w_ref, o_ref, wb_ref, *, act):
    _cast_weights(w_ref, wb_ref)
    acc = jnp.dot(a_ref[...], wb_ref[...], preferred_element_type=jnp.float32)
    if act == "silu":
        acc = _silu(acc)
    o_ref[...] = acc.astype(o_ref.dtype)


def _mm(a, w, col0, n, *, act=None, out_dtype=jnp.bfloat16, bm=1024, bn=1024, name="mm"):
    t, k = a.shape
    cb = col0 // bn
    return pl.pallas_call(
        functools.partial(_mm_kernel, act=act),
        grid=(n // bn, t // bm),
        in_specs=[pl.BlockSpec((bm, k), lambda j, i: (i, 0)),
                  pl.BlockSpec((k, bn), lambda j, i: (0, cb + j))],
        out_specs=pl.BlockSpec((bm, bn), lambda j, i: (i, j)),
        out_shape=jax.ShapeDtypeStruct((t, n), out_dtype),
        scratch_shapes=[pltpu.VMEM((k, bn), jnp.bfloat16)],
        compiler_params=_cparams("parallel", "arbitrary"),
        name=name,
    )(a, w)


def _gate_kernel(a_ref, w_ref, lbraw_ref, o_ref, wb_ref, *, layer):
    _cast_weights(w_ref, wb_ref)
    x = jnp.dot(a_ref[...], wb_ref[...], preferred_element_type=jnp.float32)
    raw = lbraw_ref[...]
    e = jnp.exp(raw - jnp.max(raw, axis=0, keepdims=True))
    p = e / jnp.sum(e, axis=0, keepdims=True)
    cs = p[0:1]
    first = cs
    for r in range(1, layer + 1):
        cs = cs + p[r:r + 1]
    lb = cs - first
    f = lb + (1.0 - lb) * (1.0 / (1.0 + jnp.exp(-x)))
    o_ref[...] = jnp.log2(f)


def _mm_gate(a, w, col0, n, lbraw, layer, *, bm=1024, bn=1024, name="mm_gate"):
    t, k = a.shape
    cb = col0 // bn
    nl = lbraw.shape[0]
    return pl.pallas_call(
        functools.partial(_gate_kernel, layer=layer),
        grid=(n // bn, t // bm),
        in_specs=[pl.BlockSpec((bm, k), lambda j, i: (i, 0)),
                  pl.BlockSpec((k, bn), lambda j, i: (0, cb + j)),
                  pl.BlockSpec((nl, bn), lambda j, i: (0, j))],
        out_specs=pl.BlockSpec((bm, bn), lambda j, i: (i, j)),
        out_shape=jax.ShapeDtypeStruct((t, n), jnp.float32),
        scratch_shapes=[pltpu.VMEM((k, bn), jnp.bfloat16)],
        compiler_params=_cparams("parallel", "arbitrary"),
        name=name,
    )(a, w, lbraw)


def _mm_mul_kernel(a_ref, w1_ref, w2_ref, o_ref, wb1_ref, wb2_ref):
    _cast_weights(w1_ref, wb1_ref)
    _cast_weights(w2_ref, wb2_ref)
    a = a_ref[...]
    c = jnp.dot(a, wb1_ref[...], preferred_element_type=jnp.float32)
    x = jnp.dot(a, wb2_ref[...], preferred_element_type=jnp.float32)
    o_ref[...] = (c * x).astype(o_ref.dtype)


def _mm_mul(a, w, col1, col2, n, *, bm=1024, bn=512, name="mm_mul"):
    t, k = a.shape
    c1, c2 = col1 // bn, col2 // bn
    return pl.pallas_call(
        _mm_mul_kernel,
        grid=(n // bn, t // bm),
        in_specs=[pl.BlockSpec((bm, k), lambda j, i: (i, 0)),
                  pl.BlockSpec((k, bn), lambda j, i: (0, c1 + j)),
                  pl.BlockSpec((k, bn), lambda j, i: (0, c2 + j))],
        out_specs=pl.BlockSpec((bm, bn), lambda j, i: (i, j)),
        out_shape=jax.ShapeDtypeStruct((t, n), jnp.bfloat16),
        scratch_shapes=[pltpu.VMEM((k, bn), jnp.bfloat16), pltpu.VMEM((k, bn), jnp.bfloat16)],
        compiler_params=_cparams("parallel", "arbitrary"),
        name=name,
    )(a, w, w)


def _mm_res_kernel(a_ref, w_ref, x_ref, wpost_ref, wnext_ref, xo_ref, *rest, nk, emit_h):
    h_ref = rest[0] if emit_h else None
    acc_ref = rest[-1] if nk > 1 else None

    def finish(m):
        xn = x_ref[...] + _rms(m, wpost_ref[...])
        xo_ref[...] = xn
        if emit_h:
            h_ref[...] = _rms(xn, wnext_ref[...]).astype(h_ref.dtype)

    part = jnp.dot(a_ref[...], w_ref[...], preferred_element_type=jnp.float32)
    if nk == 1:
        finish(part)
        return
    kk = pl.program_id(1)

    @pl.when(kk == 0)
    def _():
        acc_ref[...] = jnp.zeros_like(acc_ref)

    acc_ref[...] += part

    @pl.when(kk == nk - 1)
    def _():
        finish(acc_ref[...])


def _mm_res(a, w, x, wpost, wnext, *, bm=512, bk=None, name="mm_res"):
    t, k = a.shape
    d = w.shape[1]
    bk = k if bk is None else bk
    nk = k // bk
    emit_h = wnext is not None
    row = pl.BlockSpec((bm, d), lambda i, kk: (i, 0))
    vec = pl.BlockSpec((1, d), lambda i, kk: (0, 0))
    out_shape = [jax.ShapeDtypeStruct((t, d), jnp.float32)]
    if emit_h:
        out_shape.append(jax.ShapeDtypeStruct((t, d), jnp.bfloat16))
    else:
        wnext = wpost
    res = pl.pallas_call(
        functools.partial(_mm_res_kernel, nk=nk, emit_h=emit_h),
        grid=(t // bm, nk),
        in_specs=[pl.BlockSpec((bm, bk), lambda i, kk: (i, kk)),
                  pl.BlockSpec((bk, d), lambda i, kk: (kk, 0)),
                  row, vec, vec],
        out_specs=[row] * len(out_shape),
        out_shape=out_shape,
        scratch_shapes=[pltpu.VMEM((bm, d), jnp.float32)] if nk > 1 else [],
        compiler_params=_cparams("parallel", "arbitrary"),
        name=name,
    )(a, w, x, wpost.reshape(1, d), wnext.reshape(1, d))
    return (res[0], res[1]) if emit_h else (res[0], None)


def _conv3(x, w_ref):
    n = x.shape[0]
    row = lax.broadcasted_iota(jnp.int32, x.shape, 0)
    prev = jnp.where(row == 0, 0.0, pltpu.roll(x, 1, 0))
    nxt = jnp.where(row == n - 1, 0.0, pltpu.roll(x, n - 1, 0))
    return w_ref[0:1, :] * prev + w_ref[1:2, :] * x + w_ref[2:3, :] * nxt


def _sconv_mid_kernel(b_ref, p_ref, w_ref, o_ref):
    z = _conv3(p_ref[0].astype(jnp.float32), w_ref)
    o_ref[0] = (b_ref[0].astype(jnp.float32) * z).astype(o_ref.dtype)


def _sconv_mid(gate_b, p, w_conv, bc=256):
    b, l, c = p.shape
    blk = pl.BlockSpec((1, l, bc), lambda i, j: (i, 0, j))
    return pl.pallas_call(
        _sconv_mid_kernel,
        grid=(b, c // bc),
        in_specs=[blk, blk, pl.BlockSpec((3, bc), lambda i, j: (0, j))],
        out_specs=blk,
        out_shape=jax.ShapeDtypeStruct((b, l, c), jnp.bfloat16),
        compiler_params=_cparams("parallel", "parallel"),
        name="sconv_mid",
    )(gate_b, p, w_conv)


def _ffn_mid_kernel(g_ref, v_ref, wg_ref, wv_ref, o_ref):
    g = _conv3(g_ref[0].astype(jnp.float32), wg_ref)
    v = _conv3(v_ref[0].astype(jnp.float32), wv_ref)
    o_ref[0] = (_silu(g) * v).astype(o_ref.dtype)


def _ffn_mid(up, w_conv, f, bc=256):
    b, l, _ = up.shape
    nb = f // bc
    return pl.pallas_call(
        _ffn_mid_kernel,
        grid=(b, nb),
        in_specs=[pl.BlockSpec((1, l, bc), lambda i, j: (i, 0, j)),
                  pl.BlockSpec((1, l, bc), lambda i, j: (i, 0, nb + j)),
                  pl.BlockSpec((3, bc), lambda i, j: (0, j)),
                  pl.BlockSpec((3, bc), lambda i, j: (0, nb + j))],
        out_specs=pl.BlockSpec((1, l, bc), lambda i, j: (i, 0, j)),
        out_shape=jax.ShapeDtypeStruct((b, l, f), jnp.bfloat16),
        compiler_params=_cparams("parallel", "parallel"),
        name="ffn_mid",
    )(up, up, w_conv, w_conv)


def _cumsum_rows(x, reverse):
    n, cols = x.shape
    g = 8
    x = x.reshape(n // g, g, cols)
    row = lax.broadcasted_iota(jnp.int32, x.shape, 1)
    off = 1
    while off < g:
        if reverse:
            x = x + jnp.where(row < g - off, pltpu.roll(x, g - off, 1), 0.0)
        else:
            x = x + jnp.where(row >= off, pltpu.roll(x, off, 1), 0.0)
        off *= 2
    order = range(n // g - 1, -1, -1) if reverse else range(n // g)
    edge = 0 if reverse else g - 1
    out = [None] * (n // g)
    carry = None
    for i in order:
        xi = x[i] if carry is None else x[i] + carry
        out[i] = xi
        carry = xi[edge:edge + 1]
    return jnp.concatenate(out, axis=0)


def _cross_block_operands(q, k, b, reverse):
    c = CHUNK
    zero = jnp.zeros((BAND, HEAD_DIM), jnp.float32)
    q_slabs, k_slabs = [], []
    s = c // 2
    while s >= BAND:
        for lo in range(0, c, 2 * s):
            m = lo + s if reverse else lo + s - 1
            bm = b[m:m + 1]
            q_rows, k_rows = [], []
            for r0 in range(0, c, BAND):
                rows = slice(r0, r0 + BAND)
                if not lo <= r0 < lo + 2 * s:
                    q_rows.append(zero)
                    k_rows.append(zero)
                elif (r0 >= lo + s) != reverse:
                    q_rows.append(q[rows] * jnp.exp2(b[rows] - bm))
                    k_rows.append(zero)
                else:
                    q_rows.append(zero)
                    k_rows.append(k[rows] * jnp.exp2(bm - b[rows]))
            q_slabs.append(jnp.concatenate(q_rows, axis=0))
            k_slabs.append(jnp.concatenate(k_rows, axis=0))
        s //= 2
    return (jnp.concatenate(q_slabs, axis=1).astype(jnp.bfloat16),
            jnp.concatenate(k_slabs, axis=1).astype(jnp.bfloat16))


def _chunk_operands(q, lf, wpad_ref, reverse):
    c = CHUNK
    b = _cumsum_rows(lf, reverse)
    f = jnp.exp2(lf)
    k = 1.0 - f
    b_last = b[0:1] if reverse else b[c - 1:c]
    q_op, k_op = _cross_block_operands(q, k, b, reverse)
    w = k
    xs = [(q * w).astype(jnp.bfloat16)]
    for d in range(1, BAND):
        wpad_ref[d - 1, BAND:BAND + c, :] = w
        s0 = BAND + 1 if reverse else BAND - 1
        w = f * wpad_ref[d - 1, s0:s0 + c, :]
        xs.append((q * w).astype(jnp.bfloat16))
    return dict(
        qe=(q * jnp.exp2(b)).astype(jnp.bfloat16),
        kdec=(k * jnp.exp2(b_last - b)).astype(jnp.bfloat16),
        decay=jnp.exp2(b_last),
        q_op=q_op, k_op=k_op, x=jnp.concatenate(xs, axis=1))


def _hgrn_kernel(q_ref, lff_ref, lfb_ref, v_ref, gs_ref, gw_ref, wself_ref, wselb_ref, o_ref,
                 of_ref, ob_ref, sf_ref, sb_ref, wpf_ref, wpb_ref):
    l = q_ref.shape[1]
    nchunk = l // CHUNK
    sf_ref[...] = jnp.zeros_like(sf_ref)
    sb_ref[...] = jnp.zeros_like(sb_ref)
    wpf_ref[...] = jnp.zeros_like(wpf_ref)
    wpb_ref[...] = jnp.zeros_like(wpb_ref)

    r_i = lax.broadcasted_iota(jnp.int32, (CHUNK, CHUNK), 0)
    c_i = lax.broadcasted_iota(jnp.int32, (CHUNK, CHUNK), 1)
    same = (r_i // BAND) == (c_i // BAND)
    mask_f = same & (c_i <= r_i)
    mask_b = same & (c_i >= r_i)
    nt = (((1,), (1,)), ((), ()))
    tn = (((0,), (0,)), ((), ()))
    f32 = jnp.float32

    def body(ci, carry):
        chains = []
        for u in range(UNROLL):
            cf = ci * UNROLL + u
            for reverse in (False, True):
                r0 = pl.multiple_of((nchunk - 1 - cf if reverse else cf) * CHUNK, CHUNK)
                rows = pl.ds(r0, CHUNK)
                lf_ref, wp_ref = (lfb_ref, wpb_ref) if reverse else (lff_ref, wpf_ref)
                ch = _chunk_operands(q_ref[0, rows, :].astype(f32), lf_ref[0, rows, :],
                                     wp_ref.at[u], reverse)
                ch.update(rows=rows, reverse=reverse, v=v_ref[0, rows, :])
                chains.append(ch)
        for reverse in (False, True):
            group = [ch for ch in chains if ch["reverse"] == reverse]
            wsel = wselb_ref[...] if reverse else wself_ref[...]
            bands = jnp.dot(jnp.concatenate([ch["x"] for ch in group], axis=0), wsel,
                            preferred_element_type=f32)
            for i, ch in enumerate(group):
                ch["bands"] = bands[i * CHUNK:(i + 1) * CHUNK]
        for ch in chains:
            ch["cross"] = lax.dot_general(ch["q_op"], ch["k_op"], nt, preferred_element_type=f32)
            ch["kv"] = lax.dot_general(ch["v"], ch["kdec"], tn, preferred_element_type=f32)
        for ch in chains:
            bands = pltpu.roll(ch["bands"], 0, 1, stride=1, stride_axis=0)[:, :CHUNK]
            scores = jnp.where(mask_b if ch["reverse"] else mask_f, bands, ch["cross"])
            ch["intra"] = jnp.dot(scores.astype(jnp.bfloat16), ch["v"], preferred_element_type=f32)
        state = {False: sf_ref[...], True: sb_ref[...]}
        for ch in chains:
            st = state[ch["reverse"]]
            inter = lax.dot_general(ch["qe"], st.astype(jnp.bfloat16), nt, preferred_element_type=f32)
            (ob_ref if ch["reverse"] else of_ref)[ch["rows"], :] = inter + ch["intra"]
            state[ch["reverse"]] = st * ch["decay"] + ch["kv"]
        sf_ref[...] = state[False]
        sb_ref[...] = state[True]
        return carry

    lax.fori_loop(0, nchunk // UNROLL, body, 0)

    def fin(ci, carry):
        rows = pl.ds(pl.multiple_of(ci * FIN_ROWS, FIN_ROWS), FIN_ROWS)
        o = of_ref[rows, :] + ob_ref[rows, :]
        y = _rms(o, gw_ref[...]) * gs_ref[0, rows, :].astype(jnp.float32)
        o_ref[0, rows, :] = y.astype(o_ref.dtype)
        return carry

    lax.fori_loop(0, l // FIN_ROWS, fin, 0)


def _band_selectors():
    d = jnp.arange(BAND * HEAD_DIM, dtype=jnp.int32) // HEAD_DIM
    lane = jnp.arange(V7X_LANES, dtype=jnp.int32)
    fwd = (lane[None, :] == ((V7X_LANES - d) % V7X_LANES)[:, None]).astype(jnp.bfloat16)
    bwd = (lane[None, :] == d[:, None]).astype(jnp.bfloat16)
    return fwd, bwd


def _hgrn_recurrence(q, lff, lfb, v, gs, gw):
    b, l, d = q.shape
    hd = HEAD_DIM
    blk = pl.BlockSpec((1, l, hd), lambda i, j: (i, 0, j))
    sel = pl.BlockSpec((BAND * hd, V7X_LANES), lambda i, j: (0, 0))
    f32 = jnp.float32
    wsel_f, wsel_b = _band_selectors()
    return pl.pallas_call(
        _hgrn_kernel,
        grid=(b, d // hd),
        in_specs=[blk, blk, blk, blk, blk, pl.BlockSpec((1, hd), lambda i, j: (0, 0)), sel, sel],
        out_specs=blk,
        out_shape=jax.ShapeDtypeStruct((b, l, d), jnp.bfloat16),
        scratch_shapes=[pltpu.VMEM((l, hd), f32), pltpu.VMEM((l, hd), f32),
                        pltpu.VMEM((hd, hd), f32), pltpu.VMEM((hd, hd), f32),
                        pltpu.VMEM((UNROLL, BAND - 1, CHUNK + 2 * BAND, hd), f32),
                        pltpu.VMEM((UNROLL, BAND - 1, CHUNK + 2 * BAND, hd), f32)],
        compiler_params=_cparams("parallel", "parallel"),
        name="hgrn_recurrence",
    )(q, lff, lfb, v, gs, gw.reshape(1, hd), wsel_f, wsel_b)


def kernel(x, hgrn_w_in, hgrn_w_out, hgrn_gnorm, hgrn_lower_bounds, sconv_w_in, sconv_w_conv, sconv_w_out,
           ffn_w_in, ffn_w_conv, ffn_w_out, norm_pre_mix, norm_post_mix, norm_pre_ffn, norm_post_ffn):
    bsz, seq, d = x.shape
    t = bsz * seq
    depth = norm_pre_mix.shape[0]
    ffn_dim = ffn_w_out.shape[1]
    bf16 = jnp.bfloat16
    f32 = jnp.float32

    xf = x.reshape(t, d)
    h = _norm(xf, norm_pre_mix[0])
    for i in range(depth):
        j = i // 2
        if i % 2 == 0:
            w_in = hgrn_w_in[j]
            q = _mm(h, w_in, 0, d, act="silu", name="hgrn_q")
            lff = _mm_gate(h, w_in, d, d, hgrn_lower_bounds[0], j, name="hgrn_gate_f")
            lfb = _mm_gate(h, w_in, 2 * d, d, hgrn_lower_bounds[1], j, name="hgrn_gate_b")
            v = _mm(h, w_in, 3 * d, d, name="hgrn_v")
            gs = _mm(h, w_in, 4 * d, d, act="silu", name="hgrn_g")
            sh = (bsz, seq, d)
            o = _hgrn_recurrence(q.reshape(sh), lff.reshape(sh), lfb.reshape(sh), v.reshape(sh),
                                 gs.reshape(sh), hgrn_gnorm[j])
            a, w_out = o.reshape(t, d), hgrn_w_out[j]
        else:
            w_in = sconv_w_in[j]
            gate_b = _mm(h, w_in, 0, d, name="sconv_b")
            p = _mm_mul(h, w_in, d, 2 * d, d, name="sconv_cx")
            sh = (bsz, seq, d)
            y = _sconv_mid(gate_b.reshape(sh), p.reshape(sh), sconv_w_conv[j])
            a, w_out = y.reshape(t, d), sconv_w_out[j]
        xf, h = _mm_res(a, w_out.astype(bf16), xf, norm_post_mix[i], norm_pre_ffn[i], name="mix_out")

        up = _mm(h, ffn_w_in[i], 0, 2 * ffn_dim, name="ffn_in")
        a = _ffn_mid(up.reshape(bsz, seq, 2 * ffn_dim), ffn_w_conv[i], ffn_dim)
        w_next = norm_pre_mix[i + 1] if i + 1 < depth else None
        xf, h = _mm_res(a.reshape(t, ffn_dim), ffn_w_out[i].astype(bf16), xf, norm_post_ffn[i], w_next,
                        bk=FFN_OUT_BK, name="ffn_out")
    return xf.reshape(bsz, seq, d)
```

```python
import functools

import jax
import jax.numpy as jnp
from jax import lax
from jax.experimental import pallas as pl
from jax.experimental.pallas import tpu as pltpu

D_MODEL = 2048
HEADS = 16
HEAD_DIM = 128
EPS = 1e-6

V7X_LANES = 128
V7X_VMEM_LIMIT_BYTES = 56 * 1024 * 1024

CHUNK = 64
BAND = 8
UNROLL = 8
FIN_ROWS = 512
HALO = 16
MM_SUB = 4
CONV_SUB = 8


def _cparams(*sem):
    return pltpu.CompilerParams(dimension_semantics=sem, vmem_limit_bytes=V7X_VMEM_LIMIT_BYTES)


def _silu(x):
    return x * (1.0 / (1.0 + jnp.exp(-x)))


def _rms(x, w):
    return x * lax.rsqrt(jnp.mean(x * x, axis=-1, keepdims=True) + EPS) * w


def _norm_kernel(x_ref, w_ref, h_ref):
    h_ref[...] = _rms(x_ref[...], w_ref[...]).astype(h_ref.dtype)


def _norm(x, w, bm=512):
    t, d = x.shape
    return pl.pallas_call(
        _norm_kernel,
        grid=(t // bm,),
        in_specs=[pl.BlockSpec((bm, d), lambda i: (i, 0)), pl.BlockSpec((1, d), lambda i: (0, 0))],
        out_specs=pl.BlockSpec((bm, d), lambda i: (i, 0)),
        out_shape=jax.ShapeDtypeStruct((t, d), jnp.bfloat16),
        compiler_params=_cparams("parallel"),
        name="rmsnorm",
    )(x, w.reshape(1, d))


def _cast_weights(w_ref, wb_ref):
    @pl.when(pl.program_id(1) == 0)
    def _():
        wb_ref[...] = w_ref[...].astype(wb_ref.dtype)


def _project_rows(a_ref, wb_refs, o_ref, epilogue):
    sb = a_ref.shape[0] // MM_SUB
    pending = None
    for r in range(MM_SUB):
        rows = slice(r * sb, (r + 1) * sb)
        accs = [jnp.dot(a_ref[rows, :], wb[...], preferred_element_type=jnp.float32) for wb in wb_refs]
        if pending is not None:
            o_ref[pending[0], :] = epilogue(*pending[1]).astype(o_ref.dtype)
        pending = (rows, accs)
    o_ref[pending[0], :] = epilogue(*pending[1]).astype(o_ref.dtype)


def _mm_kernel(a_ref, w_ref, o_ref, wb_ref, *, act):
    _cast_weights(w_ref, wb_ref)
    _project_rows(a_ref, [wb_ref], o_ref, _silu if act == "silu" else (lambda acc: acc))


def _mm(a, w, layer, col0, n, *, act=None, out_dtype=jnp.bfloat16, bm=1024, bn=1024, name="mm"):
    t, k = a.shape
    cb = col0 // bn
    return pl.pallas_call(
        functools.partial(_mm_kernel, act=act),
        grid=(n // bn, t // bm),
        in_specs=[pl.BlockSpec((bm, k), lambda j, i: (i, 0)),
                  pl.BlockSpec((None, k, bn), lambda j, i: (layer, 0, cb + j))],
        out_specs=pl.BlockSpec((bm, bn), lambda j, i: (i, j)),
        out_shape=jax.ShapeDtypeStruct((t, n), out_dtype),
        scratch_shapes=[pltpu.VMEM((k, bn), jnp.bfloat16)],
        compiler_params=_cparams("parallel", "arbitrary"),
        name=name,
    )(a, w)


def _gate_kernel(a_ref, w_ref, lbraw_ref, o_ref, wb_ref, *, layer):
    _cast_weights(w_ref, wb_ref)
    raw = lbraw_ref[...]
    e = jnp.exp(raw - jnp.max(raw, axis=0, keepdims=True))
    p = e / jnp.sum(e, axis=0, keepdims=True)
    cs = p[0:1]
    first = cs
    for r in range(1, layer + 1):
        cs = cs + p[r:r + 1]
    lb = cs - first

    def log2_gate(x):
        return jnp.log2(lb + (1.0 - lb) * (1.0 / (1.0 + jnp.exp(-x))))

    _project_rows(a_ref, [wb_ref], o_ref, log2_gate)


def _mm_gate(a, w, layer, col0, n, lbraw, direction, *, bm=1024, bn=1024, name="mm_gate"):
    t, k = a.shape
    cb = col0 // bn
    nl = lbraw.shape[1]
    return pl.pallas_call(
        functools.partial(_gate_kernel, layer=layer),
        grid=(n // bn, t // bm),
        in_specs=[pl.BlockSpec((bm, k), lambda j, i: (i, 0)),
                  pl.BlockSpec((None, k, bn), lambda j, i: (layer, 0, cb + j)),
                  pl.BlockSpec((None, nl, bn), lambda j, i: (direction, 0, j))],
        out_specs=pl.BlockSpec((bm, bn), lambda j, i: (i, j)),
        out_shape=jax.ShapeDtypeStruct((t, n), jnp.float32),
        scratch_shapes=[pltpu.VMEM((k, bn), jnp.bfloat16)],
        compiler_params=_cparams("parallel", "arbitrary"),
        name=name,
    )(a, w, lbraw)


def _mm_res_kernel(a_ref, w_ref, x_ref, wpost_ref, wnext_ref, xo_ref, *h_ref):
    m = jnp.dot(a_ref[...], w_ref[...], preferred_element_type=jnp.float32)
    xn = x_ref[...] + _rms(m, wpost_ref[...])
    xo_ref[...] = xn
    if h_ref:
        h_ref[0][...] = _rms(xn, wnext_ref[...]).astype(h_ref[0].dtype)


def _mm_res(a, w, layer, x, wpost, wnext, *, bm, name="mm_res"):
    t, k = a.shape
    d = w.shape[2]
    emit_h = wnext is not None
    row = pl.BlockSpec((bm, d), lambda i: (i, 0))
    vec = pl.BlockSpec((1, d), lambda i: (0, 0))
    out_shape = [jax.ShapeDtypeStruct((t, d), jnp.float32)]
    if emit_h:
        out_shape.append(jax.ShapeDtypeStruct((t, d), jnp.bfloat16))
    else:
        wnext = wpost
    res = pl.pallas_call(
        _mm_res_kernel,
        grid=(t // bm,),
        in_specs=[pl.BlockSpec((bm, k), lambda i: (i, 0)),
                  pl.BlockSpec((None, k, d), lambda i: (layer, 0, 0), pipeline_mode=pl.Buffered(1)),
                  row, vec, vec],
        out_specs=[row] * len(out_shape),
        out_shape=out_shape,
        compiler_params=_cparams("parallel"),
        name=name,
    )(a, w, x, wpost.reshape(1, d), wnext.reshape(1, d))
    return (res[0], res[1]) if emit_h else (res[0], None)


def _conv_proj_kernel(ap_ref, a_ref, an_ref, *refs, mode, tiles_per_seq):
    nw, nc = (2, 2) if mode == "ffn" else (3, 1)
    w_refs, c_refs, o_ref = refs[:nw], refs[nw:nw + nc], refs[nw + nc]
    wb_refs, u_refs = refs[nw + nc + 1:2 * nw + nc + 1], refs[2 * nw + nc + 1:]
    for w_ref, wb_ref in zip(w_refs, wb_refs):
        _cast_weights(w_ref, wb_ref)
    bm = a_ref.shape[0]
    sb = bm // CONV_SUB
    pos = pl.program_id(1) % tiles_per_seq
    keep_prev = (pos != 0).astype(jnp.float32)
    keep_next = (pos != tiles_per_seq - 1).astype(jnp.float32)
    splits = [0] + [HALO + r * sb for r in range(1, CONV_SUB)] + [bm + 2 * HALO]

    def project(r):
        rows = slice(splits[r], splits[r + 1])
        lhs = a_ref[r * sb:(r + 1) * sb, :]
        if r == 0:
            lhs = jnp.concatenate([ap_ref[...], lhs], axis=0)
        if r == CONV_SUB - 1:
            lhs = jnp.concatenate([lhs, an_ref[...]], axis=0)
        accs = [jnp.dot(lhs, wb_ref[...], preferred_element_type=jnp.float32) for wb_ref in wb_refs]
        vals = accs if mode == "ffn" else [accs[0], accs[1] * accs[2]]
        for u_ref, val in zip(u_refs, vals):
            u_ref[rows, :] = val
            if r == 0:
                u_ref[HALO - 1:HALO, :] = u_ref[HALO - 1:HALO, :] * keep_prev
            if r == CONV_SUB - 1:
                u_ref[HALO + bm:HALO + bm + 1, :] = u_ref[HALO + bm:HALO + bm + 1, :] * keep_next

    def conv(u_ref, c_ref, r0):
        return (c_ref[0:1, :] * u_ref[r0 - 1:r0 - 1 + sb, :] + c_ref[1:2, :] * u_ref[r0:r0 + sb, :]
                + c_ref[2:3, :] * u_ref[r0 + 1:r0 + 1 + sb, :])

    def finish(r):
        r0 = HALO + r * sb
        if mode == "ffn":
            y = _silu(conv(u_refs[0], c_refs[0], r0)) * conv(u_refs[1], c_refs[1], r0)
        else:
            y = u_refs[0][r0:r0 + sb, :] * conv(u_refs[1], c_refs[0], r0)
        o_ref[r * sb:(r + 1) * sb, :] = y.astype(o_ref.dtype)

    project(0)
    for r in range(1, CONV_SUB):
        project(r)
        finish(r - 1)
    finish(CONV_SUB - 1)


def _conv_proj(a, w, w_conv, layer, n, seq, mode, *, bm=1024, bn=512, name="conv_proj"):
    t, k = a.shape
    nw, nc = (2, 2) if mode == "ffn" else (3, 1)
    nb = n // bn
    hb = bm // HALO
    last = t // HALO - 1

    def seg(s, rows):
        return pl.BlockSpec((None, rows, bn), lambda j, i: (layer, 0, s * nb + j))

    return pl.pallas_call(
        functools.partial(_conv_proj_kernel, mode=mode, tiles_per_seq=seq // bm),
        grid=(nb, t // bm),
        in_specs=[pl.BlockSpec((HALO, k), lambda j, i: (jnp.maximum(i * hb - 1, 0), 0)),
                  pl.BlockSpec((bm, k), lambda j, i: (i, 0)),
                  pl.BlockSpec((HALO, k), lambda j, i: (jnp.minimum((i + 1) * hb, last), 0))]
                 + [seg(s, k) for s in range(nw)] + [seg(s, 3) for s in range(nc)],
        out_specs=pl.BlockSpec((bm, bn), lambda j, i: (i, j)),
        out_shape=jax.ShapeDtypeStruct((t, n), jnp.bfloat16),
        scratch_shapes=[pltpu.VMEM((k, bn), jnp.bfloat16)] * nw
                       + [pltpu.VMEM((bm + 2 * HALO, bn), jnp.float32)] * 2,
        compiler_params=_cparams("parallel", "arbitrary"),
        name=name,
    )(a, a, a, *([w] * nw), *([w_conv] * nc))


def _cumsum_rows(x, reverse):
    n, cols = x.shape
    g = 8
    x = x.reshape(n // g, g, cols)
    row = lax.broadcasted_iota(jnp.int32, x.shape, 1)
    off = 1
    while off < g:
        if reverse:
            x = x + jnp.where(row < g - off, pltpu.roll(x, g - off, 1), 0.0)
        else:
            x = x + jnp.where(row >= off, pltpu.roll(x, off, 1), 0.0)
        off *= 2
    order = range(n // g - 1, -1, -1) if reverse else range(n // g)
    edge = 0 if reverse else g - 1
    out = [None] * (n // g)
    carry = None
    for i in order:
        xi = x[i] if carry is None else x[i] + carry
        out[i] = xi
        carry = xi[edge:edge + 1]
    return jnp.concatenate(out, axis=0)


def _cross_block_operands(q, k, b, reverse):
    c = CHUNK
    zero = jnp.zeros((BAND, HEAD_DIM), jnp.float32)
    q_slabs, k_slabs = [], []
    s = c // 2
    while s >= BAND:
        for lo in range(0, c, 2 * s):
            m = lo + s if reverse else lo + s - 1
            bm = b[m:m + 1]
            q_rows, k_rows = [], []
            for r0 in range(0, c, BAND):
                rows = slice(r0, r0 + BAND)
                if not lo <= r0 < lo + 2 * s:
                    q_rows.append(zero)
                    k_rows.append(zero)
                elif (r0 >= lo + s) != reverse:
                    q_rows.append(q[rows] * jnp.exp2(b[rows] - bm))
                    k_rows.append(zero)
                else:
                    q_rows.append(zero)
                    k_rows.append(k[rows] * jnp.exp2(bm - b[rows]))
            q_slabs.append(jnp.concatenate(q_rows, axis=0))
            k_slabs.append(jnp.concatenate(k_rows, axis=0))
        s //= 2
    return (jnp.concatenate(q_slabs, axis=1).astype(jnp.bfloat16),
            jnp.concatenate(k_slabs, axis=1).astype(jnp.bfloat16))


def _chunk_operands(q, lf, wpad_ref, reverse):
    c = CHUNK
    b = _cumsum_rows(lf, reverse)
    f = jnp.exp2(lf)
    k = 1.0 - f
    b_last = b[0:1] if reverse else b[c - 1:c]
    q_op, k_op = _cross_block_operands(q, k, b, reverse)
    w = k
    xs = [(q * w).astype(jnp.bfloat16)]
    for d in range(1, BAND):
        wpad_ref[d - 1, BAND:BAND + c, :] = w
        s0 = BAND + 1 if reverse else BAND - 1
        w = f * wpad_ref[d - 1, s0:s0 + c, :]
        xs.append((q * w).astype(jnp.bfloat16))
    return dict(
        qe=(q * jnp.exp2(b)).astype(jnp.bfloat16),
        kdec=(k * jnp.exp2(b_last - b)).astype(jnp.bfloat16),
        decay=jnp.exp2(b_last),
        q_op=q_op, k_op=k_op, x=jnp.concatenate(xs, axis=1))


def _hgrn_kernel(q_ref, lff_ref, lfb_ref, v_ref, gs_ref, gw_ref, wself_ref, wselb_ref, o_ref,
                 of_ref, ob_ref, sf_ref, sb_ref, wpf_ref, wpb_ref):
    l = q_ref.shape[1]
    nchunk = l // CHUNK
    sf_ref[...] = jnp.zeros_like(sf_ref)
    sb_ref[...] = jnp.zeros_like(sb_ref)
    wpf_ref[...] = jnp.zeros_like(wpf_ref)
    wpb_ref[...] = jnp.zeros_like(wpb_ref)

    r_i = lax.broadcasted_iota(jnp.int32, (CHUNK, CHUNK), 0)
    c_i = lax.broadcasted_iota(jnp.int32, (CHUNK, CHUNK), 1)
    same = (r_i // BAND) == (c_i // BAND)
    mask_f = same & (c_i <= r_i)
    mask_b = same & (c_i >= r_i)
    nt = (((1,), (1,)), ((), ()))
    tn = (((0,), (0,)), ((), ()))
    f32 = jnp.float32

    def body(ci, carry):
        chains = []
        for u in range(UNROLL):
            cf = ci * UNROLL + u
            for reverse in (False, True):
                r0 = pl.multiple_of((nchunk - 1 - cf if reverse else cf) * CHUNK, CHUNK)
                rows = pl.ds(r0, CHUNK)
                lf_ref, wp_ref = (lfb_ref, wpb_ref) if reverse else (lff_ref, wpf_ref)
                ch = _chunk_operands(q_ref[0, rows, :].astype(f32), lf_ref[0, rows, :],
                                     wp_ref.at[u], reverse)
                ch.update(rows=rows, reverse=reverse, v=v_ref[0, rows, :])
                chains.append(ch)
        for reverse in (False, True):
            group = [ch for ch in chains if ch["reverse"] == reverse]
            wsel = wselb_ref[...] if reverse else wself_ref[...]
            bands = jnp.dot(jnp.concatenate([ch["x"] for ch in group], axis=0), wsel,
                            preferred_element_type=f32)
            for i, ch in enumerate(group):
                ch["bands"] = bands[i * CHUNK:(i + 1) * CHUNK]
        for ch in chains:
            ch["cross"] = lax.dot_general(ch["q_op"], ch["k_op"], nt, preferred_element_type=f32)
            ch["kv"] = lax.dot_general(ch["v"], ch["kdec"], tn, preferred_element_type=f32)
        for ch in chains:
            bands = pltpu.roll(ch["bands"], 0, 1, stride=1, stride_axis=0)[:, :CHUNK]
            scores = jnp.where(mask_b if ch["reverse"] else mask_f, bands, ch["cross"])
            ch["intra"] = jnp.dot(scores.astype(jnp.bfloat16), ch["v"], preferred_element_type=f32)
        state = {False: sf_ref[...], True: sb_ref[...]}
        for ch in chains:
            st = state[ch["reverse"]]
            inter = lax.dot_general(ch["qe"], st.astype(jnp.bfloat16), nt, preferred_element_type=f32)
            (ob_ref if ch["reverse"] else of_ref)[ch["rows"], :] = inter + ch["intra"]
            state[ch["reverse"]] = st * ch["decay"] + ch["kv"]
        sf_ref[...] = state[False]
        sb_ref[...] = state[True]
        return carry

    lax.fori_loop(0, nchunk // UNROLL, body, 0)

    def fin(ci, carry):
        rows = pl.ds(pl.multiple_of(ci * FIN_ROWS, FIN_ROWS), FIN_ROWS)
        o = of_ref[rows, :] + ob_ref[rows, :]
        y = _rms(o, gw_ref[...]) * gs_ref[0, rows, :].astype(jnp.float32)
        o_ref[0, rows, :] = y.astype(o_ref.dtype)
        return carry

    lax.fori_loop(0, l // FIN_ROWS, fin, 0)


def _band_selectors():
    d = jnp.arange(BAND * HEAD_DIM, dtype=jnp.int32) // HEAD_DIM
    lane = jnp.arange(V7X_LANES, dtype=jnp.int32)
    fwd = (lane[None, :] == ((V7X_LANES - d) % V7X_LANES)[:, None]).astype(jnp.bfloat16)
    bwd = (lane[None, :] == d[:, None]).astype(jnp.bfloat16)
    return fwd, bwd


def _hgrn_recurrence(q, lff, lfb, v, gs, gw):
    b, l, d = q.shape
    hd = HEAD_DIM
    blk = pl.BlockSpec((1, l, hd), lambda i, j: (i, 0, j))
    sel = pl.BlockSpec((BAND * hd, V7X_LANES), lambda i, j: (0, 0))
    f32 = jnp.float32
    wsel_f, wsel_b = _band_selectors()
    return pl.pallas_call(
        _hgrn_kernel,
        grid=(b, d // hd),
        in_specs=[blk, blk, blk, blk, blk, pl.BlockSpec((1, hd), lambda i, j: (0, 0)), sel, sel],
        out_specs=blk,
        out_shape=jax.ShapeDtypeStruct((b, l, d), jnp.bfloat16),
        scratch_shapes=[pltpu.VMEM((l, hd), f32), pltpu.VMEM((l, hd), f32),
                        pltpu.VMEM((hd, hd), f32), pltpu.VMEM((hd, hd), f32),
                        pltpu.VMEM((UNROLL, BAND - 1, CHUNK + 2 * BAND, hd), f32),
                        pltpu.VMEM((UNROLL, BAND - 1, CHUNK + 2 * BAND, hd), f32)],
        compiler_params=_cparams("parallel", "parallel"),
        name="hgrn_recurrence",
    )(q, lff, lfb, v, gs, gw.reshape(1, hd), wsel_f, wsel_b)


def kernel(x, hgrn_w_in, hgrn_w_out, hgrn_gnorm, hgrn_lower_bounds, sconv_w_in, sconv_w_conv, sconv_w_out,
           ffn_w_in, ffn_w_conv, ffn_w_out, norm_pre_mix, norm_post_mix, norm_pre_ffn, norm_post_ffn):
    bsz, seq, d = x.shape
    t = bsz * seq
    depth = norm_pre_mix.shape[0]
    ffn_dim = ffn_w_out.shape[1]
    bf16 = jnp.bfloat16
    sh = (bsz, seq, d)

    xf = x.reshape(t, d)
    h = _norm(xf, norm_pre_mix[0])
    hgrn_w_out_b, sconv_w_out_b, ffn_w_out_b = (w.astype(bf16) for w in (hgrn_w_out, sconv_w_out, ffn_w_out))
    for i in range(depth):
        j = i // 2
        if i % 2 == 0:
            q = _mm(h, hgrn_w_in, j, 0, d, act="silu", name="hgrn_q")
            lff = _mm_gate(h, hgrn_w_in, j, d, d, hgrn_lower_bounds, 0, name="hgrn_gate_f")
            lfb = _mm_gate(h, hgrn_w_in, j, 2 * d, d, hgrn_lower_bounds, 1, name="hgrn_gate_b")
            v = _mm(h, hgrn_w_in, j, 3 * d, d, name="hgrn_v")
            gs = _mm(h, hgrn_w_in, j, 4 * d, d, act="silu", name="hgrn_g")
            o = _hgrn_recurrence(q.reshape(sh), lff.reshape(sh), lfb.reshape(sh), v.reshape(sh),
                                 gs.reshape(sh), hgrn_gnorm[j])
            a, w_out = o.reshape(t, d), hgrn_w_out_b
        else:
            a = _conv_proj(h, sconv_w_in, sconv_w_conv, j, d, seq, "sconv", name="sconv_in")
            w_out = sconv_w_out_b
        xf, h = _mm_res(a, w_out, j, xf, norm_post_mix[i], norm_pre_ffn[i], bm=512, name="mix_out")

        a = _conv_proj(h, ffn_w_in, ffn_w_conv, i, ffn_dim, seq, "ffn", name="ffn_in")
        w_next = norm_pre_mix[i + 1] if i + 1 < depth else None
        xf, h = _mm_res(a, ffn_w_out_b, i, xf, norm_post_ffn[i], w_next, bm=256, name="ffn_out")
    return xf.reshape(bsz, seq, d)
```

```python
import functools

import jax
import jax.numpy as jnp
from jax import lax
from jax.experimental import pallas as pl
from jax.experimental.pallas import tpu as pltpu

D_MODEL = 2048
HEADS = 16
HEAD_DIM = 128
EPS = 1e-6

V7X_LANES = 128
V7X_VMEM_LIMIT_BYTES = 56 * 1024 * 1024

CHUNK = 64
BAND = 8
UNROLL = 8
FIN_ROWS = 512
HALO = 16
MM_SUB = 4
CONV_SUB = 4


def _cparams(*sem):
    return pltpu.CompilerParams(dimension_semantics=sem, vmem_limit_bytes=V7X_VMEM_LIMIT_BYTES)


def _silu(x):
    return x * (1.0 / (1.0 + jnp.exp(-x)))


def _rms(x, w):
    return x * lax.rsqrt(jnp.mean(x * x, axis=-1, keepdims=True) + EPS) * w


def _norm_kernel(x_ref, w_ref, h_ref):
    h_ref[...] = _rms(x_ref[...], w_ref[...]).astype(h_ref.dtype)


def _norm(x, w, bm=512):
    t, d = x.shape
    return pl.pallas_call(
        _norm_kernel,
        grid=(t // bm,),
        in_specs=[pl.BlockSpec((bm, d), lambda i: (i, 0)), pl.BlockSpec((1, d), lambda i: (0, 0))],
        out_specs=pl.BlockSpec((bm, d), lambda i: (i, 0)),
        out_shape=jax.ShapeDtypeStruct((t, d), jnp.bfloat16),
        compiler_params=_cparams("parallel"),
        name="rmsnorm",
    )(x, w.reshape(1, d))


def _cast_weights(w_ref, wb_ref):
    @pl.when(pl.program_id(1) == 0)
    def _():
        wb_ref[...] = w_ref[...].astype(wb_ref.dtype)


def _project_rows(a_ref, wb_refs, o_ref, epilogue):
    sb = a_ref.shape[0] // MM_SUB
    pending = None
    for r in range(MM_SUB):
        rows = slice(r * sb, (r + 1) * sb)
        accs = [jnp.dot(a_ref[rows, :], wb[...], preferred_element_type=jnp.float32) for wb in wb_refs]
        if pending is not None:
            o_ref[pending[0], :] = epilogue(*pending[1]).astype(o_ref.dtype)
        pending = (rows, accs)
    o_ref[pending[0], :] = epilogue(*pending[1]).astype(o_ref.dtype)


def _mm_kernel(a_ref, w_ref, o_ref, wb_ref, *, act):
    _cast_weights(w_ref, wb_ref)
    _project_rows(a_ref, [wb_ref], o_ref, _silu if act == "silu" else (lambda acc: acc))


def _mm(a, w, layer, col0, n, *, act=None, out_dtype=jnp.bfloat16, bm=1024, bn=1024, name="mm"):
    t, k = a.shape
    cb = col0 // bn
    return pl.pallas_call(
        functools.partial(_mm_kernel, act=act),
        grid=(n // bn, t // bm),
        in_specs=[pl.BlockSpec((bm, k), lambda j, i: (i, 0)),
                  pl.BlockSpec((None, k, bn), lambda j, i: (layer, 0, cb + j))],
        out_specs=pl.BlockSpec((bm, bn), lambda j, i: (i, j)),
        out_shape=jax.ShapeDtypeStruct((t, n), out_dtype),
        scratch_shapes=[pltpu.VMEM((k, bn), jnp.bfloat16)],
        compiler_params=_cparams("parallel", "arbitrary"),
        name=name,
    )(a, w)


def _gate_kernel(a_ref, w_ref, lbraw_ref, o_ref, wb_ref, *, layer):
    _cast_weights(w_ref, wb_ref)
    raw = lbraw_ref[...]
    e = jnp.exp(raw - jnp.max(raw, axis=0, keepdims=True))
    p = e / jnp.sum(e, axis=0, keepdims=True)
    cs = p[0:1]
    first = cs
    for r in range(1, layer + 1):
        cs = cs + p[r:r + 1]
    lb = cs - first

    def log2_gate(x):
        return jnp.log2(lb + (1.0 - lb) * (1.0 / (1.0 + jnp.exp(-x))))

    _project_rows(a_ref, [wb_ref], o_ref, log2_gate)


def _mm_gate(a, w, layer, col0, n, lbraw, direction, *, bm=1024, bn=1024, name="mm_gate"):
    t, k = a.shape
    cb = col0 // bn
    nl = lbraw.shape[1]
    return pl.pallas_call(
        functools.partial(_gate_kernel, layer=layer),
        grid=(n // bn, t // bm),
        in_specs=[pl.BlockSpec((bm, k), lambda j, i: (i, 0)),
                  pl.BlockSpec((None, k, bn), lambda j, i: (layer, 0, cb + j)),
                  pl.BlockSpec((None, nl, bn), lambda j, i: (direction, 0, j))],
        out_specs=pl.BlockSpec((bm, bn), lambda j, i: (i, j)),
        out_shape=jax.ShapeDtypeStruct((t, n), jnp.float32),
        scratch_shapes=[pltpu.VMEM((k, bn), jnp.bfloat16)],
        compiler_params=_cparams("parallel", "arbitrary"),
        name=name,
    )(a, w, lbraw)


def _mm_res_kernel(a_ref, w_ref, x_ref, wpost_ref, wnext_ref, xo_ref, *h_ref):
    m = jnp.dot(a_ref[...], w_ref[...], preferred_element_type=jnp.float32)
    xn = x_ref[...] + _rms(m, wpost_ref[...])
    xo_ref[...] = xn
    if h_ref:
        h_ref[0][...] = _rms(xn, wnext_ref[...]).astype(h_ref[0].dtype)


def _mm_res(a, w, layer, x, wpost, wnext, *, bm, name="mm_res"):
    t, k = a.shape
    d = w.shape[2]
    emit_h = wnext is not None
    row = pl.BlockSpec((bm, d), lambda i: (i, 0))
    vec = pl.BlockSpec((1, d), lambda i: (0, 0))
    out_shape = [jax.ShapeDtypeStruct((t, d), jnp.float32)]
    if emit_h:
        out_shape.append(jax.ShapeDtypeStruct((t, d), jnp.bfloat16))
    else:
        wnext = wpost
    res = pl.pallas_call(
        _mm_res_kernel,
        grid=(t // bm,),
        in_specs=[pl.BlockSpec((bm, k), lambda i: (i, 0)),
                  pl.BlockSpec((None, k, d), lambda i: (layer, 0, 0), pipeline_mode=pl.Buffered(1)),
                  row, vec, vec],
        out_specs=[row] * len(out_shape),
        out_shape=out_shape,
        compiler_params=_cparams("parallel"),
        name=name,
    )(a, w, x, wpost.reshape(1, d), wnext.reshape(1, d))
    return (res[0], res[1]) if emit_h else (res[0], None)


def _conv_proj_kernel(ap_ref, a_ref, an_ref, *refs, mode, tiles_per_seq):
    nw, nc = (2, 2) if mode == "ffn" else (3, 1)
    w_refs, c_refs, o_ref = refs[:nw], refs[nw:nw + nc], refs[nw + nc]
    wb_refs, u_refs = refs[nw + nc + 1:2 * nw + nc + 1], refs[2 * nw + nc + 1:]
    for w_ref, wb_ref in zip(w_refs, wb_refs):
        _cast_weights(w_ref, wb_ref)
    bm = a_ref.shape[0]
    sb = bm // CONV_SUB
    pos = pl.program_id(1) % tiles_per_seq
    keep_prev = (pos != 0).astype(jnp.float32)
    keep_next = (pos != tiles_per_seq - 1).astype(jnp.float32)
    splits = [0] + [HALO + r * sb for r in range(1, CONV_SUB)] + [bm + 2 * HALO]

    def project(r):
        rows = slice(splits[r], splits[r + 1])
        lhs = a_ref[r * sb:(r + 1) * sb, :]
        if r == 0:
            lhs = jnp.concatenate([ap_ref[...], lhs], axis=0)
        if r == CONV_SUB - 1:
            lhs = jnp.concatenate([lhs, an_ref[...]], axis=0)
        accs = [jnp.dot(lhs, wb_ref[...], preferred_element_type=jnp.float32) for wb_ref in wb_refs]
        vals = accs if mode == "ffn" else [accs[0], accs[1] * accs[2]]
        for u_ref, val in zip(u_refs, vals):
            u_ref[rows, :] = val
            if r == 0:
                u_ref[HALO - 1:HALO, :] = u_ref[HALO - 1:HALO, :] * keep_prev
            if r == CONV_SUB - 1:
                u_ref[HALO + bm:HALO + bm + 1, :] = u_ref[HALO + bm:HALO + bm + 1, :] * keep_next

    def conv(u_ref, c_ref, r0):
        g = 8
        ue = u_ref[r0 - g:r0 + sb + g, :]
        n = sb + 2 * g
        prev = pltpu.roll(ue, 1, 0)[g:g + sb]
        nxt = pltpu.roll(ue, n - 1, 0)[g:g + sb]
        return c_ref[0:1, :] * prev + c_ref[1:2, :] * ue[g:g + sb] + c_ref[2:3, :] * nxt

    def finish(r):
        r0 = HALO + r * sb
        if mode == "ffn":
            y = _silu(conv(u_refs[0], c_refs[0], r0)) * conv(u_refs[1], c_refs[1], r0)
        else:
            y = u_refs[0][r0:r0 + sb, :] * conv(u_refs[1], c_refs[0], r0)
        o_ref[r * sb:(r + 1) * sb, :] = y.astype(o_ref.dtype)

    project(0)
    for r in range(1, CONV_SUB):
        project(r)
        finish(r - 1)
    finish(CONV_SUB - 1)


def _conv_proj(a, w, w_conv, layer, n, seq, mode, *, bm=1024, bn=512, name="conv_proj"):
    t, k = a.shape
    nw, nc = (2, 2) if mode == "ffn" else (3, 1)
    nb = n // bn
    hb = bm // HALO
    last = t // HALO - 1

    def seg(s, rows):
        return pl.BlockSpec((None, rows, bn), lambda j, i: (layer, 0, s * nb + j))

    return pl.pallas_call(
        functools.partial(_conv_proj_kernel, mode=mode, tiles_per_seq=seq // bm),
        grid=(nb, t // bm),
        in_specs=[pl.BlockSpec((HALO, k), lambda j, i: (jnp.maximum(i * hb - 1, 0), 0)),
                  pl.BlockSpec((bm, k), lambda j, i: (i, 0)),
                  pl.BlockSpec((HALO, k), lambda j, i: (jnp.minimum((i + 1) * hb, last), 0))]
                 + [seg(s, k) for s in range(nw)] + [seg(s, 3) for s in range(nc)],
        out_specs=pl.BlockSpec((bm, bn), lambda j, i: (i, j)),
        out_shape=jax.ShapeDtypeStruct((t, n), jnp.bfloat16),
        scratch_shapes=[pltpu.VMEM((k, bn), jnp.bfloat16)] * nw
                       + [pltpu.VMEM((bm + 2 * HALO, bn), jnp.float32)] * 2,
        compiler_params=_cparams("parallel", "arbitrary"),
        name=name,
    )(a, a, a, *([w] * nw), *([w_conv] * nc))


def _cumsum_rows(x, reverse):
    n, cols = x.shape
    g = 8
    x = x.reshape(n // g, g, cols)
    row = lax.broadcasted_iota(jnp.int32, x.shape, 1)
    off = 1
    while off < g:
        if reverse:
            x = x + jnp.where(row < g - off, pltpu.roll(x, g - off, 1), 0.0)
        else:
            x = x + jnp.where(row >= off, pltpu.roll(x, off, 1), 0.0)
        off *= 2
    order = range(n // g - 1, -1, -1) if reverse else range(n // g)
    edge = 0 if reverse else g - 1
    out = [None] * (n // g)
    carry = None
    for i in order:
        xi = x[i] if carry is None else x[i] + carry
        out[i] = xi
        carry = xi[edge:edge + 1]
    return jnp.concatenate(out, axis=0)


def _cross_block_operands(q, k, b, reverse):
    c = CHUNK
    zero = jnp.zeros((BAND, HEAD_DIM), jnp.float32)
    q_slabs, k_slabs = [], []
    s = c // 2
    while s >= BAND:
        for lo in range(0, c, 2 * s):
            m = lo + s if reverse else lo + s - 1
            bm = b[m:m + 1]
            q_rows, k_rows = [], []
            for r0 in range(0, c, BAND):
                rows = slice(r0, r0 + BAND)
                if not lo <= r0 < lo + 2 * s:
                    q_rows.append(zero)
                    k_rows.append(zero)
                elif (r0 >= lo + s) != reverse:
                    q_rows.append(q[rows] * jnp.exp2(b[rows] - bm))
                    k_rows.append(zero)
                else:
                    q_rows.append(zero)
                    k_rows.append(k[rows] * jnp.exp2(bm - b[rows]))
            q_slabs.append(jnp.concatenate(q_rows, axis=0))
            k_slabs.append(jnp.concatenate(k_rows, axis=0))
        s //= 2
    return (jnp.concatenate(q_slabs, axis=1).astype(jnp.bfloat16),
            jnp.concatenate(k_slabs, axis=1).astype(jnp.bfloat16))


def _chunk_operands(q, lf, wpad_ref, reverse):
    c = CHUNK
    b = _cumsum_rows(lf, reverse)
    f = jnp.exp2(lf)
    k = 1.0 - f
    b_last = b[0:1] if reverse else b[c - 1:c]
    q_op, k_op = _cross_block_operands(q, k, b, reverse)
    w = k
    xs = [(q * w).astype(jnp.bfloat16)]
    for d in range(1, BAND):
        wpad_ref[d - 1, BAND:BAND + c, :] = w
        s0 = BAND + 1 if reverse else BAND - 1
        w = f * wpad_ref[d - 1, s0:s0 + c, :]
        xs.append((q * w).astype(jnp.bfloat16))
    return dict(
        qe=(q * jnp.exp2(b)).astype(jnp.bfloat16),
        kdec=(k * jnp.exp2(b_last - b)).astype(jnp.bfloat16),
        decay=jnp.exp2(b_last),
        q_op=q_op, k_op=k_op, x=jnp.concatenate(xs, axis=1))


def _hgrn_kernel(q_ref, lff_ref, lfb_ref, v_ref, gs_ref, gw_ref, wself_ref, wselb_ref, o_ref,
                 of_ref, ob_ref, sf_ref, sb_ref, wpf_ref, wpb_ref):
    l = q_ref.shape[1]
    nchunk = l // CHUNK
    sf_ref[...] = jnp.zeros_like(sf_ref)
    sb_ref[...] = jnp.zeros_like(sb_ref)
    for wp_ref in (wpf_ref, wpb_ref):
        margin = jnp.zeros(wp_ref.shape[:2] + (BAND, wp_ref.shape[3]), wp_ref.dtype)
        wp_ref[:, :, 0:BAND, :] = margin
        wp_ref[:, :, BAND + CHUNK:, :] = margin

    r_i = lax.broadcasted_iota(jnp.int32, (CHUNK, CHUNK), 0)
    c_i = lax.broadcasted_iota(jnp.int32, (CHUNK, CHUNK), 1)
    same = (r_i // BAND) == (c_i // BAND)
    mask_f = same & (c_i <= r_i)
    mask_b = same & (c_i >= r_i)
    nt = (((1,), (1,)), ((), ()))
    tn = (((0,), (0,)), ((), ()))
    f32 = jnp.float32

    def body(ci, carry):
        chains = []
        for u in range(UNROLL):
            cf = ci * UNROLL + u
            for reverse in (False, True):
                r0 = pl.multiple_of((nchunk - 1 - cf if reverse else cf) * CHUNK, CHUNK)
                rows = pl.ds(r0, CHUNK)
                lf_ref, wp_ref = (lfb_ref, wpb_ref) if reverse else (lff_ref, wpf_ref)
                ch = _chunk_operands(q_ref[0, rows, :].astype(f32), lf_ref[0, rows, :],
                                     wp_ref.at[u], reverse)
                ch.update(rows=rows, reverse=reverse, v=v_ref[0, rows, :])
                chains.append(ch)
        for reverse in (False, True):
            group = [ch for ch in chains if ch["reverse"] == reverse]
            wsel = wselb_ref[...] if reverse else wself_ref[...]
            bands = jnp.dot(jnp.concatenate([ch["x"] for ch in group], axis=0), wsel,
                            preferred_element_type=f32)
            for i, ch in enumerate(group):
                ch["bands"] = bands[i * CHUNK:(i + 1) * CHUNK]
        for ch in chains:
            ch["cross"] = lax.dot_general(ch["q_op"], ch["k_op"], nt, preferred_element_type=f32)
            ch["kv"] = lax.dot_general(ch["v"], ch["kdec"], tn, preferred_element_type=f32)
        for ch in chains:
            bands = pltpu.roll(ch["bands"], 0, 1, stride=1, stride_axis=0)[:, :CHUNK]
            scores = jnp.where(mask_b if ch["reverse"] else mask_f, bands, ch["cross"])
            ch["intra"] = jnp.dot(scores.astype(jnp.bfloat16), ch["v"], preferred_element_type=f32)
        state = {False: sf_ref[...], True: sb_ref[...]}
        for ch in chains:
            st = state[ch["reverse"]]
            inter = lax.dot_general(ch["qe"], st.astype(jnp.bfloat16), nt, preferred_element_type=f32)
            (ob_ref if ch["reverse"] else of_ref)[ch["rows"], :] = inter + ch["intra"]
            state[ch["reverse"]] = st * ch["decay"] + ch["kv"]
        sf_ref[...] = state[False]
        sb_ref[...] = state[True]
        return carry

    lax.fori_loop(0, nchunk // UNROLL, body, 0)

    def fin(ci, carry):
        rows = pl.ds(pl.multiple_of(ci * FIN_ROWS, FIN_ROWS), FIN_ROWS)
        o = of_ref[rows, :] + ob_ref[rows, :]
        y = _rms(o, gw_ref[...]) * gs_ref[0, rows, :].astype(jnp.float32)
        o_ref[0, rows, :] = y.astype(o_ref.dtype)
        return carry

    lax.fori_loop(0, l // FIN_ROWS, fin, 0)


def _band_selectors():
    d = jnp.arange(BAND * HEAD_DIM, dtype=jnp.int32) // HEAD_DIM
    lane = jnp.arange(V7X_LANES, dtype=jnp.int32)
    fwd = (lane[None, :] == ((V7X_LANES - d) % V7X_LANES)[:, None]).astype(jnp.bfloat16)
    bwd = (lane[None, :] == d[:, None]).astype(jnp.bfloat16)
    return fwd, bwd


def _hgrn_recurrence(q, lff, lfb, v, gs, gw):
    b, l, d = q.shape
    hd = HEAD_DIM
    blk = pl.BlockSpec((1, l, hd), lambda i, j: (i, 0, j))
    sel = pl.BlockSpec((BAND * hd, V7X_LANES), lambda i, j: (0, 0))
    f32 = jnp.float32
    wsel_f, wsel_b = _band_selectors()
    return pl.pallas_call(
        _hgrn_kernel,
        grid=(b, d // hd),
        in_specs=[blk, blk, blk, blk, blk, pl.BlockSpec((1, hd), lambda i, j: (0, 0)), sel, sel],
        out_specs=blk,
        out_shape=jax.ShapeDtypeStruct((b, l, d), jnp.bfloat16),
        scratch_shapes=[pltpu.VMEM((l, hd), f32), pltpu.VMEM((l, hd), f32),
                        pltpu.VMEM((hd, hd), f32), pltpu.VMEM((hd, hd), f32),
                        pltpu.VMEM((UNROLL, BAND - 1, CHUNK + 2 * BAND, hd), f32),
                        pltpu.VMEM((UNROLL, BAND - 1, CHUNK + 2 * BAND, hd), f32)],
        compiler_params=_cparams("parallel", "parallel"),
        name="hgrn_recurrence",
    )(q, lff, lfb, v, gs, gw.reshape(1, hd), wsel_f, wsel_b)


def kernel(x, hgrn_w_in, hgrn_w_out, hgrn_gnorm, hgrn_lower_bounds, sconv_w_in, sconv_w_conv, sconv_w_out,
           ffn_w_in, ffn_w_conv, ffn_w_out, norm_pre_mix, norm_post_mix, norm_pre_ffn, norm_post_ffn):
    bsz, seq, d = x.shape
    t = bsz * seq
    depth = norm_pre_mix.shape[0]
    ffn_dim = ffn_w_out.shape[1]
    bf16 = jnp.bfloat16
    sh = (bsz, seq, d)

    xf = x.reshape(t, d)
    h = _norm(xf, norm_pre_mix[0])
    hgrn_w_out_b, sconv_w_out_b, ffn_w_out_b = (w.astype(bf16) for w in (hgrn_w_out, sconv_w_out, ffn_w_out))
    for i in range(depth):
        j = i // 2
        if i % 2 == 0:
            q = _mm(h, hgrn_w_in, j, 0, d, act="silu", name="hgrn_q")
            lff = _mm_gate(h, hgrn_w_in, j, d, d, hgrn_lower_bounds, 0, name="hgrn_gate_f")
            lfb = _mm_gate(h, hgrn_w_in, j, 2 * d, d, hgrn_lower_bounds, 1, name="hgrn_gate_b")
            v = _mm(h, hgrn_w_in, j, 3 * d, d, name="hgrn_v")
            gs = _mm(h, hgrn_w_in, j, 4 * d, d, act="silu", name="hgrn_g")
            o = _hgrn_recurrence(q.reshape(sh), lff.reshape(sh), lfb.reshape(sh), v.reshape(sh),
                                 gs.reshape(sh), hgrn_gnorm[j])
            a, w_out = o.reshape(t, d), hgrn_w_out_b
        else:
            a = _conv_proj(h, sconv_w_in, sconv_w_conv, j, d, seq, "sconv", name="sconv_in")
            w_out = sconv_w_out_b
        xf, h = _mm_res(a, w_out, j, xf, norm_post_mix[i], norm_pre_ffn[i], bm=512, name="mix_out")

        a = _conv_proj(h, ffn_w_in, ffn_w_conv, i, ffn_dim, seq, "ffn", name="ffn_in")
        w_next = norm_pre_mix[i + 1] if i + 1 < depth else None
        xf, h = _mm_res(a, ffn_w_out_b, i, xf, norm_post_ffn[i], w_next, bm=256, name="ffn_out")
    return xf.reshape(bsz, seq, d)
```

```python
import functools

import jax
import jax.numpy as jnp
from jax import lax
from jax.experimental import pallas as pl
from jax.experimental.pallas import tpu as pltpu

D_MODEL = 2048
HEADS = 16
HEAD_DIM = 128
EPS = 1e-6

V7X_LANES = 128
V7X_VMEM_LIMIT_BYTES = 56 * 1024 * 1024

CHUNK = 64
BAND = 8
UNROLL = 16
FIN_ROWS = 512
HALO = 16
MM_SUB = 4
CONV_SUB_ROWS = 256


def _cparams(*sem):
    return pltpu.CompilerParams(dimension_semantics=sem, vmem_limit_bytes=V7X_VMEM_LIMIT_BYTES)


def _silu(x):
    return x * (1.0 / (1.0 + jnp.exp(-x)))


def _rms(x, w):
    return x * lax.rsqrt(jnp.mean(x * x, axis=-1, keepdims=True) + EPS) * w


def _norm_kernel(x_ref, w_ref, h_ref):
    h_ref[...] = _rms(x_ref[...], w_ref[...]).astype(h_ref.dtype)


def _norm(x, w, bm=512):
    t, d = x.shape
    return pl.pallas_call(
        _norm_kernel,
        grid=(t // bm,),
        in_specs=[pl.BlockSpec((bm, d), lambda i: (i, 0)), pl.BlockSpec((1, d), lambda i: (0, 0))],
        out_specs=pl.BlockSpec((bm, d), lambda i: (i, 0)),
        out_shape=jax.ShapeDtypeStruct((t, d), jnp.bfloat16),
        compiler_params=_cparams("parallel"),
        name="rmsnorm",
    )(x, w.reshape(1, d))


def _cast_weights(w_ref, wb_ref):
    @pl.when(pl.program_id(1) == 0)
    def _():
        wb_ref[...] = w_ref[...].astype(wb_ref.dtype)


def _project_rows(a_ref, wb_refs, o_ref, epilogue):
    sb = a_ref.shape[0] // MM_SUB
    pending = None
    for r in range(MM_SUB):
        rows = slice(r * sb, (r + 1) * sb)
        accs = [jnp.dot(a_ref[rows, :], wb[...], preferred_element_type=jnp.float32) for wb in wb_refs]
        if pending is not None:
            o_ref[pending[0], :] = epilogue(*pending[1]).astype(o_ref.dtype)
        pending = (rows, accs)
    o_ref[pending[0], :] = epilogue(*pending[1]).astype(o_ref.dtype)


def _mm_kernel(a_ref, w_ref, o_ref, wb_ref, *, act):
    _cast_weights(w_ref, wb_ref)
    _project_rows(a_ref, [wb_ref], o_ref, _silu if act == "silu" else (lambda acc: acc))


def _mm(a, w, layer, col0, n, *, act=None, out_dtype=jnp.bfloat16, bm=1024, bn=1024, name="mm"):
    t, k = a.shape
    cb = col0 // bn
    return pl.pallas_call(
        functools.partial(_mm_kernel, act=act),
        grid=(n // bn, t // bm),
        in_specs=[pl.BlockSpec((bm, k), lambda j, i: (i, 0)),
                  pl.BlockSpec((None, k, bn), lambda j, i: (layer, 0, cb + j))],
        out_specs=pl.BlockSpec((bm, bn), lambda j, i: (i, j)),
        out_shape=jax.ShapeDtypeStruct((t, n), out_dtype),
        scratch_shapes=[pltpu.VMEM((k, bn), jnp.bfloat16)],
        compiler_params=_cparams("parallel", "arbitrary"),
        name=name,
    )(a, w)


def _gate_kernel(a_ref, w_ref, lbraw_ref, o_ref, wb_ref, *, layer):
    _cast_weights(w_ref, wb_ref)
    raw = lbraw_ref[...]
    e = jnp.exp(raw - jnp.max(raw, axis=0, keepdims=True))
    p = e / jnp.sum(e, axis=0, keepdims=True)
    cs = p[0:1]
    first = cs
    for r in range(1, layer + 1):
        cs = cs + p[r:r + 1]
    lb = cs - first

    def log2_gate(x):
        return jnp.log2(lb + (1.0 - lb) * (1.0 / (1.0 + jnp.exp(-x))))

    _project_rows(a_ref, [wb_ref], o_ref, log2_gate)


def _mm_gate(a, w, layer, col0, n, lbraw, direction, *, bm=1024, bn=1024, name="mm_gate"):
    t, k = a.shape
    cb = col0 // bn
    nl = lbraw.shape[1]
    return pl.pallas_call(
        functools.partial(_gate_kernel, layer=layer),
        grid=(n // bn, t // bm),
        in_specs=[pl.BlockSpec((bm, k), lambda j, i: (i, 0)),
                  pl.BlockSpec((None, k, bn), lambda j, i: (layer, 0, cb + j)),
                  pl.BlockSpec((None, nl, bn), lambda j, i: (direction, 0, j))],
        out_specs=pl.BlockSpec((bm, bn), lambda j, i: (i, j)),
        out_shape=jax.ShapeDtypeStruct((t, n), jnp.float32),
        scratch_shapes=[pltpu.VMEM((k, bn), jnp.bfloat16)],
        compiler_params=_cparams("parallel", "arbitrary"),
        name=name,
    )(a, w, lbraw)


def _mm_res_kernel(a_ref, w_ref, x_ref, wpost_ref, wnext_ref, xo_ref, *h_ref):
    m = jnp.dot(a_ref[...], w_ref[...], preferred_element_type=jnp.float32)
    xn = x_ref[...] + _rms(m, wpost_ref[...])
    xo_ref[...] = xn
    if h_ref:
        h_ref[0][...] = _rms(xn, wnext_ref[...]).astype(h_ref[0].dtype)


def _mm_res(a, w, layer, x, wpost, wnext, *, bm, name="mm_res"):
    t, k = a.shape
    d = w.shape[2]
    emit_h = wnext is not None
    row = pl.BlockSpec((bm, d), lambda i: (i, 0))
    vec = pl.BlockSpec((1, d), lambda i: (0, 0))
    out_shape = [jax.ShapeDtypeStruct((t, d), jnp.float32)]
    if emit_h:
        out_shape.append(jax.ShapeDtypeStruct((t, d), jnp.bfloat16))
    else:
        wnext = wpost
    res = pl.pallas_call(
        _mm_res_kernel,
        grid=(t // bm,),
        in_specs=[pl.BlockSpec((bm, k), lambda i: (i, 0)),
                  pl.BlockSpec((None, k, d), lambda i: (layer, 0, 0), pipeline_mode=pl.Buffered(1)),
                  row, vec, vec],
        out_specs=[row] * len(out_shape),
        out_shape=out_shape,
        compiler_params=_cparams("parallel"),
        name=name,
    )(a, w, x, wpost.reshape(1, d), wnext.reshape(1, d))
    return (res[0], res[1]) if emit_h else (res[0], None)


def _conv_proj_kernel(ap_ref, a_ref, an_ref, *refs, mode, tiles_per_seq):
    nw, nc = (2, 2) if mode == "ffn" else (3, 1)
    w_refs, c_refs, o_ref = refs[:nw], refs[nw:nw + nc], refs[nw + nc]
    wb_ref, u_refs = refs[nw + nc + 1], refs[nw + nc + 2:]
    bn = o_ref.shape[1]

    @pl.when(pl.program_id(1) == 0)
    def _():
        for s, w_ref in enumerate(w_refs):
            wb_ref[:, s * bn:(s + 1) * bn] = w_ref[...].astype(wb_ref.dtype)

    bm = a_ref.shape[0]
    sb = CONV_SUB_ROWS
    nsub = bm // sb
    pos = pl.program_id(1) % tiles_per_seq
    keep_prev = (pos != 0).astype(jnp.float32)
    keep_next = (pos != tiles_per_seq - 1).astype(jnp.float32)
    splits = [0] + [HALO + r * sb for r in range(1, nsub)] + [bm + 2 * HALO]

    def project(r):
        rows = slice(splits[r], splits[r + 1])
        lhs = a_ref[r * sb:(r + 1) * sb, :]
        if r == 0:
            lhs = jnp.concatenate([ap_ref[...], lhs], axis=0)
        if r == nsub - 1:
            lhs = jnp.concatenate([lhs, an_ref[...]], axis=0)
        acc = jnp.dot(lhs, wb_ref[...], preferred_element_type=jnp.float32)
        accs = [acc[:, s * bn:(s + 1) * bn] for s in range(nw)]
        vals = accs if mode == "ffn" else [accs[0], accs[1] * accs[2]]
        for u_ref, val in zip(u_refs, vals):
            u_ref[rows, :] = val
            if r == 0:
                u_ref[HALO - 1:HALO, :] = u_ref[HALO - 1:HALO, :] * keep_prev
            if r == nsub - 1:
                u_ref[HALO + bm:HALO + bm + 1, :] = u_ref[HALO + bm:HALO + bm + 1, :] * keep_next

    def conv(u_ref, c_ref, r0):
        g = 8
        ue = u_ref[r0 - g:r0 + sb + g, :]
        n = sb + 2 * g
        prev = pltpu.roll(ue, 1, 0)[g:g + sb]
        nxt = pltpu.roll(ue, n - 1, 0)[g:g + sb]
        return c_ref[0:1, :] * prev + c_ref[1:2, :] * ue[g:g + sb] + c_ref[2:3, :] * nxt

    def finish(r):
        r0 = HALO + r * sb
        if mode == "ffn":
            y = _silu(conv(u_refs[0], c_refs[0], r0)) * conv(u_refs[1], c_refs[1], r0)
        else:
            y = u_refs[0][r0:r0 + sb, :] * conv(u_refs[1], c_refs[0], r0)
        o_ref[r * sb:(r + 1) * sb, :] = y.astype(o_ref.dtype)

    project(0)
    for r in range(1, nsub):
        project(r)
        finish(r - 1)
    finish(nsub - 1)


def _conv_proj(a, w, w_conv, layer, n, seq, mode, *, bm=1024, bn=512, name="conv_proj"):
    t, k = a.shape
    nw, nc = (2, 2) if mode == "ffn" else (3, 1)
    nb = n // bn
    hb = bm // HALO
    last = t // HALO - 1

    def seg(s, rows):
        return pl.BlockSpec((None, rows, bn), lambda j, i: (layer, 0, s * nb + j))

    return pl.pallas_call(
        functools.partial(_conv_proj_kernel, mode=mode, tiles_per_seq=seq // bm),
        grid=(nb, t // bm),
        in_specs=[pl.BlockSpec((HALO, k), lambda j, i: (jnp.maximum(i * hb - 1, 0), 0)),
                  pl.BlockSpec((bm, k), lambda j, i: (i, 0)),
                  pl.BlockSpec((HALO, k), lambda j, i: (jnp.minimum((i + 1) * hb, last), 0))]
                 + [seg(s, k) for s in range(nw)] + [seg(s, 3) for s in range(nc)],
        out_specs=pl.BlockSpec((bm, bn), lambda j, i: (i, j)),
        out_shape=jax.ShapeDtypeStruct((t, n), jnp.bfloat16),
        scratch_shapes=[pltpu.VMEM((k, nw * bn), jnp.bfloat16)]
                       + [pltpu.VMEM((bm + 2 * HALO, bn), jnp.float32)] * 2,
        compiler_params=_cparams("parallel", "arbitrary"),
        name=name,
    )(a, a, a, *([w] * nw), *([w_conv] * nc))


def _cumsum_rows(x, reverse):
    n, cols = x.shape
    g = 8
    x = x.reshape(n // g, g, cols)
    row = lax.broadcasted_iota(jnp.int32, x.shape, 1)
    off = 1
    while off < g:
        if reverse:
            x = x + jnp.where(row < g - off, pltpu.roll(x, g - off, 1), 0.0)
        else:
            x = x + jnp.where(row >= off, pltpu.roll(x, off, 1), 0.0)
        off *= 2
    order = range(n // g - 1, -1, -1) if reverse else range(n // g)
    edge = 0 if reverse else g - 1
    out = [None] * (n // g)
    carry = None
    for i in order:
        xi = x[i] if carry is None else x[i] + carry
        out[i] = xi
        carry = xi[edge:edge + 1]
    return jnp.concatenate(out, axis=0)


def _cross_block_operands(q, k, b, reverse):
    c = CHUNK
    zero = jnp.zeros((BAND, HEAD_DIM), jnp.float32)
    q_slabs, k_slabs = [], []
    s = c // 2
    while s >= BAND:
        for lo in range(0, c, 2 * s):
            m = lo + s if reverse else lo + s - 1
            bm = b[m:m + 1]
            q_rows, k_rows = [], []
            for r0 in range(0, c, BAND):
                rows = slice(r0, r0 + BAND)
                if not lo <= r0 < lo + 2 * s:
                    q_rows.append(zero)
                    k_rows.append(zero)
                elif (r0 >= lo + s) != reverse:
                    q_rows.append(q[rows] * jnp.exp2(b[rows] - bm))
                    k_rows.append(zero)
                else:
                    q_rows.append(zero)
                    k_rows.append(k[rows] * jnp.exp2(bm - b[rows]))
            q_slabs.append(jnp.concatenate(q_rows, axis=0))
            k_slabs.append(jnp.concatenate(k_rows, axis=0))
        s //= 2
    return (jnp.concatenate(q_slabs, axis=1).astype(jnp.bfloat16),
            jnp.concatenate(k_slabs, axis=1).astype(jnp.bfloat16))


def _chunk_operands(q, lf, wpad_ref, reverse):
    c = CHUNK
    b = _cumsum_rows(lf, reverse)
    f = jnp.exp2(lf)
    k = 1.0 - f
    b_last = b[0:1] if reverse else b[c - 1:c]
    q_op, k_op = _cross_block_operands(q, k, b, reverse)
    w = k
    xs = [(q * w).astype(jnp.bfloat16)]
    for d in range(1, BAND):
        wpad_ref[d - 1, BAND:BAND + c, :] = w
        s0 = BAND + 1 if reverse else BAND - 1
        w = f * wpad_ref[d - 1, s0:s0 + c, :]
        xs.append((q * w).astype(jnp.bfloat16))
    return dict(
        qe=(q * jnp.exp2(b)).astype(jnp.bfloat16),
        kdec=(k * jnp.exp2(b_last - b)).astype(jnp.bfloat16),
        decay=jnp.exp2(b_last),
        q_op=q_op, k_op=k_op, x=jnp.concatenate(xs, axis=1))


def _hgrn_kernel(q_ref, lff_ref, lfb_ref, v_ref, gs_ref, gw_ref, wself_ref, wselb_ref, o_ref,
                 of_ref, ob_ref, sf_ref, sb_ref, wpf_ref, wpb_ref):
    l = q_ref.shape[1]
    nchunk = l // CHUNK
    sf_ref[...] = jnp.zeros_like(sf_ref)
    sb_ref[...] = jnp.zeros_like(sb_ref)
    for wp_ref in (wpf_ref, wpb_ref):
        margin = jnp.zeros(wp_ref.shape[:2] + (BAND, wp_ref.shape[3]), wp_ref.dtype)
        wp_ref[:, :, 0:BAND, :] = margin
        wp_ref[:, :, BAND + CHUNK:, :] = margin

    r_i = lax.broadcasted_iota(jnp.int32, (CHUNK, CHUNK), 0)
    c_i = lax.broadcasted_iota(jnp.int32, (CHUNK, CHUNK), 1)
    same = (r_i // BAND) == (c_i // BAND)
    mask_f = same & (c_i <= r_i)
    mask_b = same & (c_i >= r_i)
    nt = (((1,), (1,)), ((), ()))
    tn = (((0,), (0,)), ((), ()))
    f32 = jnp.float32

    def body(ci, carry):
        chains = []
        for u in range(UNROLL):
            cf = ci * UNROLL + u
            for reverse in (False, True):
                r0 = pl.multiple_of((nchunk - 1 - cf if reverse else cf) * CHUNK, CHUNK)
                rows = pl.ds(r0, CHUNK)
                lf_ref, wp_ref = (lfb_ref, wpb_ref) if reverse else (lff_ref, wpf_ref)
                ch = _chunk_operands(q_ref[0, rows, :].astype(f32), lf_ref[0, rows, :],
                                     wp_ref.at[u], reverse)
                ch.update(rows=rows, reverse=reverse, v=v_ref[0, rows, :])
                chains.append(ch)
        for reverse in (False, True):
            group = [ch for ch in chains if ch["reverse"] == reverse]
            wsel = wselb_ref[...] if reverse else wself_ref[...]
            bands = jnp.dot(jnp.concatenate([ch["x"] for ch in group], axis=0), wsel,
                            preferred_element_type=f32)
            for i, ch in enumerate(group):
                ch["bands"] = bands[i * CHUNK:(i + 1) * CHUNK]
        for ch in chains:
            ch["cross"] = lax.dot_general(ch["q_op"], ch["k_op"], nt, preferred_element_type=f32)
            ch["kv"] = lax.dot_general(ch["v"], ch["kdec"], tn, preferred_element_type=f32)
        for ch in chains:
            bands = pltpu.roll(ch["bands"], 0, 1, stride=1, stride_axis=0)[:, :CHUNK]
            scores = jnp.where(mask_b if ch["reverse"] else mask_f, bands, ch["cross"])
            ch["intra"] = jnp.dot(scores.astype(jnp.bfloat16), ch["v"], preferred_element_type=f32)
        state = {False: sf_ref[...], True: sb_ref[...]}
        for ch in chains:
            st = state[ch["reverse"]]
            inter = lax.dot_general(ch["qe"], st.astype(jnp.bfloat16), nt, preferred_element_type=f32)
            (ob_ref if ch["reverse"] else of_ref)[ch["rows"], :] = inter + ch["intra"]
            state[ch["reverse"]] = st * ch["decay"] + ch["kv"]
        sf_ref[...] = state[False]
        sb_ref[...] = state[True]
        return carry

    lax.fori_loop(0, nchunk // UNROLL, body, 0)

    def fin(ci, carry):
        rows = pl.ds(pl.multiple_of(ci * FIN_ROWS, FIN_ROWS), FIN_ROWS)
        o = of_ref[rows, :] + ob_ref[rows, :]
        y = _rms(o, gw_ref[...]) * gs_ref[0, rows, :].astype(jnp.float32)
        o_ref[0, rows, :] = y.astype(o_ref.dtype)
        return carry

    lax.fori_loop(0, l // FIN_ROWS, fin, 0)


def _band_selectors():
    d = jnp.arange(BAND * HEAD_DIM, dtype=jnp.int32) // HEAD_DIM
    lane = jnp.arange(V7X_LANES, dtype=jnp.int32)
    fwd = (lane[None, :] == ((V7X_LANES - d) % V7X_LANES)[:, None]).astype(jnp.bfloat16)
    bwd = (lane[None, :] == d[:, None]).astype(jnp.bfloat16)
    return fwd, bwd


def _hgrn_recurrence(q, lff, lfb, v, gs, gw):
    b, l, d = q.shape
    hd = HEAD_DIM
    blk = pl.BlockSpec((1, l, hd), lambda i, j: (i, 0, j))
    sel = pl.BlockSpec((BAND * hd, V7X_LANES), lambda i, j: (0, 0))
    f32 = jnp.float32
    wsel_f, wsel_b = _band_selectors()
    return pl.pallas_call(
        _hgrn_kernel,
        grid=(b, d // hd),
        in_specs=[blk, blk, blk, blk, blk, pl.BlockSpec((1, hd), lambda i, j: (0, 0)), sel, sel],
        out_specs=blk,
        out_shape=jax.ShapeDtypeStruct((b, l, d), jnp.bfloat16),
        scratch_shapes=[pltpu.VMEM((l, hd), f32), pltpu.VMEM((l, hd), f32),
                        pltpu.VMEM((hd, hd), f32), pltpu.VMEM((hd, hd), f32),
                        pltpu.VMEM((UNROLL, BAND - 1, CHUNK + 2 * BAND, hd), f32),
                        pltpu.VMEM((UNROLL, BAND - 1, CHUNK + 2 * BAND, hd), f32)],
        compiler_params=_cparams("parallel", "parallel"),
        name="hgrn_recurrence",
    )(q, lff, lfb, v, gs, gw.reshape(1, hd), wsel_f, wsel_b)


def kernel(x, hgrn_w_in, hgrn_w_out, hgrn_gnorm, hgrn_lower_bounds, sconv_w_in, sconv_w_conv, sconv_w_out,
           ffn_w_in, ffn_w_conv, ffn_w_out, norm_pre_mix, norm_post_mix, norm_pre_ffn, norm_post_ffn):
    bsz, seq, d = x.shape
    t = bsz * seq
    depth = norm_pre_mix.shape[0]
    ffn_dim = ffn_w_out.shape[1]
    bf16 = jnp.bfloat16
    sh = (bsz, seq, d)

    xf = x.reshape(t, d)
    h = _norm(xf, norm_pre_mix[0])
    hgrn_w_out_b, sconv_w_out_b, ffn_w_out_b = (w.astype(bf16) for w in (hgrn_w_out, sconv_w_out, ffn_w_out))
    for i in range(depth):
        j = i // 2
        if i % 2 == 0:
            q = _mm(h, hgrn_w_in, j, 0, d, act="silu", name="hgrn_q")
            lff = _mm_gate(h, hgrn_w_in, j, d, d, hgrn_lower_bounds, 0, name="hgrn_gate_f")
            lfb = _mm_gate(h, hgrn_w_in, j, 2 * d, d, hgrn_lower_bounds, 1, name="hgrn_gate_b")
            v = _mm(h, hgrn_w_in, j, 3 * d, d, name="hgrn_v")
            gs = _mm(h, hgrn_w_in, j, 4 * d, d, act="silu", name="hgrn_g")
            o = _hgrn_recurrence(q.reshape(sh), lff.reshape(sh), lfb.reshape(sh), v.reshape(sh),
                                 gs.reshape(sh), hgrn_gnorm[j])
            a, w_out = o.reshape(t, d), hgrn_w_out_b
        else:
            a = _conv_proj(h, sconv_w_in, sconv_w_conv, j, d, seq, "sconv", name="sconv_in")
            w_out = sconv_w_out_b
        xf, h = _mm_res(a, w_out, j, xf, norm_post_mix[i], norm_pre_ffn[i], bm=512, name="mix_out")

        a = _conv_proj(h, ffn_w_in, ffn_w_conv, i, ffn_dim, seq, "ffn", name="ffn_in")
        w_next = norm_pre_mix[i + 1] if i + 1 < depth else None
        xf, h = _mm_res(a, ffn_w_out_b, i, xf, norm_post_ffn[i], w_next, bm=256, name="ffn_out")
    return xf.reshape(bsz, seq, d)
```

```python
import functools

import jax
import jax.numpy as jnp
from jax import lax
from jax.experimental import pallas as pl
from jax.experimental.pallas import tpu as pltpu

D_MODEL = 2048
HEADS = 16
HEAD_DIM = 128
EPS = 1e-6

V7X_LANES = 128
V7X_VMEM_LIMIT_BYTES = 56 * 1024 * 1024

CHUNK = 64
BAND = 8
UNROLL = 16
FIN_ROWS = 512
HALO = 16
MM_SUB = 4
CONV_SUB_ROWS = 256


def _cparams(*sem):
    return pltpu.CompilerParams(dimension_semantics=sem, vmem_limit_bytes=V7X_VMEM_LIMIT_BYTES)


def _silu(x):
    return x * (1.0 / (1.0 + jnp.exp(-x)))


def _rms(x, w):
    return x * lax.rsqrt(jnp.mean(x * x, axis=-1, keepdims=True) + EPS) * w


def _norm_kernel(x_ref, w_ref, h_ref):
    h_ref[...] = _rms(x_ref[...], w_ref[...]).astype(h_ref.dtype)


def _norm(x, w, bm=512):
    t, d = x.shape
    return pl.pallas_call(
        _norm_kernel,
        grid=(t // bm,),
        in_specs=[pl.BlockSpec((bm, d), lambda i: (i, 0)), pl.BlockSpec((1, d), lambda i: (0, 0))],
        out_specs=pl.BlockSpec((bm, d), lambda i: (i, 0)),
        out_shape=jax.ShapeDtypeStruct((t, d), jnp.bfloat16),
        compiler_params=_cparams("parallel"),
        name="rmsnorm",
    )(x, w.reshape(1, d))


def _cast_weights(w_ref, wb_ref):
    @pl.when(pl.program_id(1) == 0)
    def _():
        wb_ref[...] = w_ref[...].astype(wb_ref.dtype)


def _project_rows(a_ref, wb_refs, o_ref, epilogue):
    sb = a_ref.shape[0] // MM_SUB
    pending = None
    for r in range(MM_SUB):
        rows = slice(r * sb, (r + 1) * sb)
        accs = [jnp.dot(a_ref[rows, :], wb[...], preferred_element_type=jnp.float32) for wb in wb_refs]
        if pending is not None:
            o_ref[pending[0], :] = epilogue(*pending[1]).astype(o_ref.dtype)
        pending = (rows, accs)
    o_ref[pending[0], :] = epilogue(*pending[1]).astype(o_ref.dtype)


def _mm_kernel(a_ref, w_ref, o_ref, wb_ref, *, act):
    _cast_weights(w_ref, wb_ref)
    _project_rows(a_ref, [wb_ref], o_ref, _silu if act == "silu" else (lambda acc: acc))


def _mm(a, w, layer, col0, n, *, act=None, out_dtype=jnp.bfloat16, bm=1024, bn=1024, name="mm"):
    t, k = a.shape
    cb = col0 // bn
    return pl.pallas_call(
        functools.partial(_mm_kernel, act=act),
        grid=(n // bn, t // bm),
        in_specs=[pl.BlockSpec((bm, k), lambda j, i: (i, 0)),
                  pl.BlockSpec((None, k, bn), lambda j, i: (layer, 0, cb + j))],
        out_specs=pl.BlockSpec((bm, bn), lambda j, i: (i, j)),
        out_shape=jax.ShapeDtypeStruct((t, n), out_dtype),
        scratch_shapes=[pltpu.VMEM((k, bn), jnp.bfloat16)],
        compiler_params=_cparams("parallel", "arbitrary"),
        name=name,
    )(a, w)


def _gate_kernel(a_ref, w_ref, lbraw_ref, o_ref, wb_ref, *, layer):
    _cast_weights(w_ref, wb_ref)
    raw = lbraw_ref[...]
    e = jnp.exp(raw - jnp.max(raw, axis=0, keepdims=True))
    p = e / jnp.sum(e, axis=0, keepdims=True)
    cs = p[0:1]
    first = cs
    for r in range(1, layer + 1):
        cs = cs + p[r:r + 1]
    lb = cs - first

    def log2_gate(x):
        return jnp.log2(lb + (1.0 - lb) * (1.0 / (1.0 + jnp.exp(-x))))

    _project_rows(a_ref, [wb_ref], o_ref, log2_gate)


def _mm_gate(a, w, layer, col0, n, lbraw, direction, *, bm=1024, bn=1024, name="mm_gate"):
    t, k = a.shape
    cb = col0 // bn
    nl = lbraw.shape[1]
    return pl.pallas_call(
        functools.partial(_gate_kernel, layer=layer),
        grid=(n // bn, t // bm),
        in_specs=[pl.BlockSpec((bm, k), lambda j, i: (i, 0)),
                  pl.BlockSpec((None, k, bn), lambda j, i: (layer, 0, cb + j)),
                  pl.BlockSpec((None, nl, bn), lambda j, i: (direction, 0, j))],
        out_specs=pl.BlockSpec((bm, bn), lambda j, i: (i, j)),
        out_shape=jax.ShapeDtypeStruct((t, n), jnp.float32),
        scratch_shapes=[pltpu.VMEM((k, bn), jnp.bfloat16)],
        compiler_params=_cparams("parallel", "arbitrary"),
        name=name,
    )(a, w, lbraw)


def _mm_res_kernel(a_ref, w_ref, x_ref, wpost_ref, wnext_ref, xo_ref, *h_ref):
    m = jnp.dot(a_ref[...], w_ref[...], preferred_element_type=jnp.float32)
    xn = x_ref[...] + _rms(m, wpost_ref[...])
    xo_ref[...] = xn
    if h_ref:
        h_ref[0][...] = _rms(xn, wnext_ref[...]).astype(h_ref[0].dtype)


def _mm_res(a, w, layer, x, wpost, wnext, *, bm, name="mm_res"):
    t, k = a.shape
    d = w.shape[2]
    emit_h = wnext is not None
    row = pl.BlockSpec((bm, d), lambda i: (i, 0))
    vec = pl.BlockSpec((1, d), lambda i: (0, 0))
    out_shape = [jax.ShapeDtypeStruct((t, d), jnp.float32)]
    if emit_h:
        out_shape.append(jax.ShapeDtypeStruct((t, d), jnp.bfloat16))
    else:
        wnext = wpost
    res = pl.pallas_call(
        _mm_res_kernel,
        grid=(t // bm,),
        in_specs=[pl.BlockSpec((bm, k), lambda i: (i, 0)),
                  pl.BlockSpec((None, k, d), lambda i: (layer, 0, 0), pipeline_mode=pl.Buffered(1)),
                  row, vec, vec],
        out_specs=[row] * len(out_shape),
        out_shape=out_shape,
        compiler_params=_cparams("parallel"),
        name=name,
    )(a, w, x, wpost.reshape(1, d), wnext.reshape(1, d))
    return (res[0], res[1]) if emit_h else (res[0], None)


def _conv_proj_kernel(ap_ref, a_ref, an_ref, *refs, mode, tiles_per_seq):
    nw, nc = (2, 2) if mode == "ffn" else (3, 1)
    w_refs, c_refs, o_ref = refs[:nw], refs[nw:nw + nc], refs[nw + nc]
    wb_refs, u_refs = refs[nw + nc + 1:2 * nw + nc + 1], refs[2 * nw + nc + 1:]
    for w_ref, wb_ref in zip(w_refs, wb_refs):
        _cast_weights(w_ref, wb_ref)
    bm = a_ref.shape[0]
    sb = CONV_SUB_ROWS
    nsub = bm // sb
    pos = pl.program_id(1) % tiles_per_seq
    keep_prev = (pos != 0).astype(jnp.float32)
    keep_next = (pos != tiles_per_seq - 1).astype(jnp.float32)
    splits = [0] + [HALO + r * sb for r in range(1, nsub)] + [bm + 2 * HALO]

    def project(r):
        rows = slice(splits[r], splits[r + 1])
        lhs = a_ref[r * sb:(r + 1) * sb, :]
        if r == 0:
            lhs = jnp.concatenate([ap_ref[...], lhs], axis=0)
        if r == nsub - 1:
            lhs = jnp.concatenate([lhs, an_ref[...]], axis=0)
        accs = [jnp.dot(lhs, wb_ref[...], preferred_element_type=jnp.float32) for wb_ref in wb_refs]
        vals = accs if mode == "ffn" else [accs[0], accs[1] * accs[2]]
        for u_ref, val in zip(u_refs, vals):
            u_ref[rows, :] = val
            if r == 0:
                u_ref[HALO - 1:HALO, :] = u_ref[HALO - 1:HALO, :] * keep_prev
            if r == nsub - 1:
                u_ref[HALO + bm:HALO + bm + 1, :] = u_ref[HALO + bm:HALO + bm + 1, :] * keep_next

    def conv(u_ref, c_ref, r0):
        g = 8
        ue = u_ref[r0 - g:r0 + sb + g, :]
        n = sb + 2 * g
        prev = pltpu.roll(ue, 1, 0)[g:g + sb]
        nxt = pltpu.roll(ue, n - 1, 0)[g:g + sb]
        return c_ref[0:1, :] * prev + c_ref[1:2, :] * ue[g:g + sb] + c_ref[2:3, :] * nxt

    def finish(r):
        r0 = HALO + r * sb
        if mode == "ffn":
            y = _silu(conv(u_refs[0], c_refs[0], r0)) * conv(u_refs[1], c_refs[1], r0)
        else:
            y = u_refs[0][r0:r0 + sb, :] * conv(u_refs[1], c_refs[0], r0)
        o_ref[r * sb:(r + 1) * sb, :] = y.astype(o_ref.dtype)

    project(0)
    for r in range(1, nsub):
        project(r)
        finish(r - 1)
    finish(nsub - 1)


def _conv_proj(a, w, w_conv, layer, n, seq, mode, *, bm=1024, bn=512, name="conv_proj"):
    t, k = a.shape
    nw, nc = (2, 2) if mode == "ffn" else (3, 1)
    nb = n // bn
    hb = bm // HALO
    last = t // HALO - 1

    def seg(s, rows):
        return pl.BlockSpec((None, rows, bn), lambda j, i: (layer, 0, s * nb + j))

    return pl.pallas_call(
        functools.partial(_conv_proj_kernel, mode=mode, tiles_per_seq=seq // bm),
        grid=(nb, t // bm),
        in_specs=[pl.BlockSpec((HALO, k), lambda j, i: (jnp.maximum(i * hb - 1, 0), 0)),
                  pl.BlockSpec((bm, k), lambda j, i: (i, 0)),
                  pl.BlockSpec((HALO, k), lambda j, i: (jnp.minimum((i + 1) * hb, last), 0))]
                 + [seg(s, k) for s in range(nw)] + [seg(s, 3) for s in range(nc)],
        out_specs=pl.BlockSpec((bm, bn), lambda j, i: (i, j)),
        out_shape=jax.ShapeDtypeStruct((t, n), jnp.bfloat16),
        scratch_shapes=[pltpu.VMEM((k, bn), jnp.bfloat16)] * nw
                       + [pltpu.VMEM((bm + 2 * HALO, bn), jnp.float32)] * 2,
        compiler_params=_cparams("parallel", "arbitrary"),
        name=name,
    )(a, a, a, *([w] * nw), *([w_conv] * nc))


def _cumsum_rows(x, reverse):
    n, cols = x.shape
    g = 8
    x = x.reshape(n // g, g, cols)
    row = lax.broadcasted_iota(jnp.int32, x.shape, 1)
    off = 1
    while off < g:
        if reverse:
            x = x + jnp.where(row < g - off, pltpu.roll(x, g - off, 1), 0.0)
        else:
            x = x + jnp.where(row >= off, pltpu.roll(x, off, 1), 0.0)
        off *= 2
    order = range(n // g - 1, -1, -1) if reverse else range(n // g)
    edge = 0 if reverse else g - 1
    out = [None] * (n // g)
    carry = None
    for i in order:
        xi = x[i] if carry is None else x[i] + carry
        out[i] = xi
        carry = xi[edge:edge + 1]
    return jnp.concatenate(out, axis=0)


def _cross_block_operands(q, k, b, reverse):
    c = CHUNK
    zero = jnp.zeros((BAND, HEAD_DIM), jnp.float32)
    q_slabs, k_slabs = [], []
    s = c // 2
    while s >= BAND:
        for lo in range(0, c, 2 * s):
            m = lo + s if reverse else lo + s - 1
            bm = b[m:m + 1]
            q_rows, k_rows = [], []
            for r0 in range(0, c, BAND):
                rows = slice(r0, r0 + BAND)
                if not lo <= r0 < lo + 2 * s:
                    q_rows.append(zero)
                    k_rows.append(zero)
                elif (r0 >= lo + s) != reverse:
                    q_rows.append(q[rows] * jnp.exp2(b[rows] - bm))
                    k_rows.append(zero)
                else:
                    q_rows.append(zero)
                    k_rows.append(k[rows] * jnp.exp2(bm - b[rows]))
            q_slabs.append(jnp.concatenate(q_rows, axis=0))
            k_slabs.append(jnp.concatenate(k_rows, axis=0))
        s //= 2
    return (jnp.concatenate(q_slabs, axis=1).astype(jnp.bfloat16),
            jnp.concatenate(k_slabs, axis=1).astype(jnp.bfloat16))


def _chunk_operands(q, lf, wpad_ref, reverse):
    c = CHUNK
    b = _cumsum_rows(lf, reverse)
    f = jnp.exp2(lf)
    k = 1.0 - f
    b_last = b[0:1] if reverse else b[c - 1:c]
    q_op, k_op = _cross_block_operands(q, k, b, reverse)
    w = k
    xs = [(q * w).astype(jnp.bfloat16)]
    for d in range(1, BAND):
        wpad_ref[d - 1, BAND:BAND + c, :] = w
        s0 = BAND + 1 if reverse else BAND - 1
        w = f * wpad_ref[d - 1, s0:s0 + c, :]
        xs.append((q * w).astype(jnp.bfloat16))
    return dict(
        qe=(q * jnp.exp2(b)).astype(jnp.bfloat16),
        kdec=(k * jnp.exp2(b_last - b)).astype(jnp.bfloat16),
        decay=jnp.exp2(b_last),
        q_op=q_op, k_op=k_op, x=jnp.concatenate(xs, axis=1))


def _hgrn_kernel(q_ref, lff_ref, lfb_ref, v_ref, gs_ref, gw_ref, wself_ref, wselb_ref, o_ref,
                 of_ref, ob_ref, sf_ref, sb_ref, wpf_ref, wpb_ref):
    l = q_ref.shape[1]
    nchunk = l // CHUNK
    sf_ref[...] = jnp.zeros_like(sf_ref)
    sb_ref[...] = jnp.zeros_like(sb_ref)
    for wp_ref in (wpf_ref, wpb_ref):
        margin = jnp.zeros(wp_ref.shape[:2] + (BAND, wp_ref.shape[3]), wp_ref.dtype)
        wp_ref[:, :, 0:BAND, :] = margin
        wp_ref[:, :, BAND + CHUNK:, :] = margin

    r_i = lax.broadcasted_iota(jnp.int32, (CHUNK, CHUNK), 0)
    c_i = lax.broadcasted_iota(jnp.int32, (CHUNK, CHUNK), 1)
    same = (r_i // BAND) == (c_i // BAND)
    mask_f = same & (c_i <= r_i)
    mask_b = same & (c_i >= r_i)
    nt = (((1,), (1,)), ((), ()))
    tn = (((0,), (0,)), ((), ()))
    f32 = jnp.float32

    def body(ci, carry):
        chains = []
        for u in range(UNROLL):
            cf = ci * UNROLL + u
            for reverse in (False, True):
                r0 = pl.multiple_of((nchunk - 1 - cf if reverse else cf) * CHUNK, CHUNK)
                rows = pl.ds(r0, CHUNK)
                lf_ref, wp_ref = (lfb_ref, wpb_ref) if reverse else (lff_ref, wpf_ref)
                ch = _chunk_operands(q_ref[0, rows, :].astype(f32), lf_ref[0, rows, :],
                                     wp_ref.at[u], reverse)
                ch.update(rows=rows, reverse=reverse, v=v_ref[0, rows, :])
                chains.append(ch)
        for reverse in (False, True):
            group = [ch for ch in chains if ch["reverse"] == reverse]
            wsel = wselb_ref[...] if reverse else wself_ref[...]
            bands = jnp.dot(jnp.concatenate([ch["x"] for ch in group], axis=0), wsel,
                            preferred_element_type=f32)
            for i, ch in enumerate(group):
                ch["bands"] = bands[i * CHUNK:(i + 1) * CHUNK]
        for ch in chains:
            ch["cross"] = lax.dot_general(ch["q_op"], ch["k_op"], nt, preferred_element_type=f32)
            ch["kv"] = lax.dot_general(ch["v"], ch["kdec"], tn, preferred_element_type=f32)
        for ch in chains:
            bands = pltpu.roll(ch["bands"], 0, 1, stride=1, stride_axis=0)[:, :CHUNK]
            scores = jnp.where(mask_b if ch["reverse"] else mask_f, bands, ch["cross"])
            ch["intra"] = jnp.dot(scores.astype(jnp.bfloat16), ch["v"], preferred_element_type=f32)
        state = {False: sf_ref[...], True: sb_ref[...]}
        for ch in chains:
            st = state[ch["reverse"]]
            inter = lax.dot_general(ch["qe"], st.astype(jnp.bfloat16), nt, preferred_element_type=f32)
            (ob_ref if ch["reverse"] else of_ref)[ch["rows"], :] = inter + ch["intra"]
            state[ch["reverse"]] = st * ch["decay"] + ch["kv"]
        sf_ref[...] = state[False]
        sb_ref[...] = state[True]
        return carry

    lax.fori_loop(0, nchunk // UNROLL, body, 0)

    def fin(ci, carry):
        rows = pl.ds(pl.multiple_of(ci * FIN_ROWS, FIN_ROWS), FIN_ROWS)
        o = of_ref[rows, :] + ob_ref[rows, :]
        y = _rms(o, gw_ref[...]) * gs_ref[0, rows, :].astype(jnp.float32)
        o_ref[0, rows, :] = y.astype(o_ref.dtype)
        return carry

    lax.fori_loop(0, l // FIN_ROWS, fin, 0)


def _band_selectors():
    d = jnp.arange(BAND * HEAD_DIM, dtype=jnp.int32) // HEAD_DIM
    lane = jnp.arange(V7X_LANES, dtype=jnp.int32)
    fwd = (lane[None, :] == ((V7X_LANES - d) % V7X_LANES)[:, None]).astype(jnp.bfloat16)
    bwd = (lane[None, :] == d[:, None]).astype(jnp.bfloat16)
    return fwd, bwd


def _hgrn_recurrence(q, lff, lfb, v, gs, gw):
    b, l, d = q.shape
    hd = HEAD_DIM
    blk = pl.BlockSpec((1, l, hd), lambda i, j: (i, 0, j))
    sel = pl.BlockSpec((BAND * hd, V7X_LANES), lambda i, j: (0, 0))
    f32 = jnp.float32
    wsel_f, wsel_b = _band_selectors()
    return pl.pallas_call(
        _hgrn_kernel,
        grid=(b, d // hd),
        in_specs=[blk, blk, blk, blk, blk, pl.BlockSpec((1, hd), lambda i, j: (0, 0)), sel, sel],
        out_specs=blk,
        out_shape=jax.ShapeDtypeStruct((b, l, d), jnp.bfloat16),
        scratch_shapes=[pltpu.VMEM((l, hd), f32), pltpu.VMEM((l, hd), f32),
                        pltpu.VMEM((hd, hd), f32), pltpu.VMEM((hd, hd), f32),
                        pltpu.VMEM((UNROLL, BAND - 1, CHUNK + 2 * BAND, hd), f32),
                        pltpu.VMEM((UNROLL, BAND - 1, CHUNK + 2 * BAND, hd), f32)],
        compiler_params=_cparams("parallel", "parallel"),
        name="hgrn_recurrence",
    )(q, lff, lfb, v, gs, gw.reshape(1, hd), wsel_f, wsel_b)


def kernel(x, hgrn_w_in, hgrn_w_out, hgrn_gnorm, hgrn_lower_bounds, sconv_w_in, sconv_w_conv, sconv_w_out,
           ffn_w_in, ffn_w_conv, ffn_w_out, norm_pre_mix, norm_post_mix, norm_pre_ffn, norm_post_ffn):
    bsz, seq, d = x.shape
    t = bsz * seq
    depth = norm_pre_mix.shape[0]
    ffn_dim = ffn_w_out.shape[1]
    bf16 = jnp.bfloat16
    sh = (bsz, seq, d)

    xf = x.reshape(t, d)
    h = _norm(xf, norm_pre_mix[0])
    hgrn_w_out_b, sconv_w_out_b, ffn_w_out_b = (w.astype(bf16) for w in (hgrn_w_out, sconv_w_out, ffn_w_out))
    for i in range(depth):
        j = i // 2
        if i % 2 == 0:
            q = _mm(h, hgrn_w_in, j, 0, d, act="silu", name="hgrn_q")
            lff = _mm_gate(h, hgrn_w_in, j, d, d, hgrn_lower_bounds, 0, name="hgrn_gate_f")
            lfb = _mm_gate(h, hgrn_w_in, j, 2 * d, d, hgrn_lower_bounds, 1, name="hgrn_gate_b")
            v = _mm(h, hgrn_w_in, j, 3 * d, d, name="hgrn_v")
            gs = _mm(h, hgrn_w_in, j, 4 * d, d, act="silu", name="hgrn_g")
            o = _hgrn_recurrence(q.reshape(sh), lff.reshape(sh), lfb.reshape(sh), v.reshape(sh),
                                 gs.reshape(sh), hgrn_gnorm[j])
            a, w_out = o.reshape(t, d), hgrn_w_out_b
        else:
            a = _conv_proj(h, sconv_w_in, sconv_w_conv, j, d, seq, "sconv", name="sconv_in")
            w_out = sconv_w_out_b
        xf, h = _mm_res(a, w_out, j, xf, norm_post_mix[i], norm_pre_ffn[i], bm=512, name="mix_out")

        a = _conv_proj(h, ffn_w_in, ffn_w_conv, i, ffn_dim, seq, "ffn", name="ffn_in")
        w_next = norm_pre_mix[i + 1] if i + 1 < depth else None
        xf, h = _mm_res(a, ffn_w_out_b, i, xf, norm_post_ffn[i], w_next, bm=256, name="ffn_out")
    return xf.reshape(bsz, seq, d)
```

```python
import functools

import jax
import jax.numpy as jnp
from jax import lax
from jax.experimental import pallas as pl
from jax.experimental.pallas import tpu as pltpu

HEAD_DIM = 128
EPS = 1e-6

V7X_LANES = 128
V7X_SUBLANES = 8
V7X_VMEM_LIMIT_BYTES = 56 * 1024 * 1024

CHUNK = 64
BAND = 8
UNROLL = 16
FIN_ROWS = 512
HALO = 16
MM_SUB = 4
CONV_SUB_ROWS = 256


def _cparams(*sem):
    return pltpu.CompilerParams(dimension_semantics=sem, vmem_limit_bytes=V7X_VMEM_LIMIT_BYTES)


def _silu(x):
    return x * (1.0 / (1.0 + jnp.exp(-x)))


def _rms(x, w):
    return x * lax.rsqrt(jnp.mean(x * x, axis=-1, keepdims=True) + EPS) * w


def _norm_kernel(x_ref, w_ref, h_ref):
    h_ref[...] = _rms(x_ref[...], w_ref[...]).astype(h_ref.dtype)


def _norm(x, w, bm=512):
    t, d = x.shape
    return pl.pallas_call(
        _norm_kernel,
        grid=(t // bm,),
        in_specs=[pl.BlockSpec((bm, d), lambda i: (i, 0)), pl.BlockSpec((1, d), lambda i: (0, 0))],
        out_specs=pl.BlockSpec((bm, d), lambda i: (i, 0)),
        out_shape=jax.ShapeDtypeStruct((t, d), jnp.bfloat16),
        compiler_params=_cparams("parallel"),
        name="rmsnorm",
    )(x, w.reshape(1, d))


def _cast_weights(w_ref, wb_ref):
    @pl.when(pl.program_id(1) == 0)
    def _():
        wb_ref[...] = w_ref[...].astype(wb_ref.dtype)


def _project_rows(a_ref, wb_refs, o_ref, epilogue):
    sb = a_ref.shape[0] // MM_SUB
    pending = None
    for r in range(MM_SUB):
        rows = slice(r * sb, (r + 1) * sb)
        accs = [jnp.dot(a_ref[rows, :], wb[...], preferred_element_type=jnp.float32) for wb in wb_refs]
        if pending is not None:
            o_ref[pending[0], :] = epilogue(*pending[1]).astype(o_ref.dtype)
        pending = (rows, accs)
    o_ref[pending[0], :] = epilogue(*pending[1]).astype(o_ref.dtype)


def _mm_kernel(a_ref, w_ref, o_ref, wb_ref, *, act):
    _cast_weights(w_ref, wb_ref)
    _project_rows(a_ref, [wb_ref], o_ref, _silu if act == "silu" else (lambda acc: acc))


def _mm(a, w, layer, col0, n, *, act=None, bm=1024, bn=1024, name="mm"):
    t, k = a.shape
    cb = col0 // bn
    return pl.pallas_call(
        functools.partial(_mm_kernel, act=act),
        grid=(n // bn, t // bm),
        in_specs=[pl.BlockSpec((bm, k), lambda j, i: (i, 0)),
                  pl.BlockSpec((None, k, bn), lambda j, i: (layer, 0, cb + j))],
        out_specs=pl.BlockSpec((bm, bn), lambda j, i: (i, j)),
        out_shape=jax.ShapeDtypeStruct((t, n), jnp.bfloat16),
        scratch_shapes=[pltpu.VMEM((k, bn), jnp.bfloat16)],
        compiler_params=_cparams("parallel", "arbitrary"),
        name=name,
    )(a, w)


def _gate_kernel(a_ref, w_ref, lbraw_ref, o_ref, wb_ref, *, layer):
    _cast_weights(w_ref, wb_ref)
    raw = lbraw_ref[...]
    e = jnp.exp(raw - jnp.max(raw, axis=0, keepdims=True))
    p = e / jnp.sum(e, axis=0, keepdims=True)
    cs = p[0:1]
    first = cs
    for r in range(1, layer + 1):
        cs = cs + p[r:r + 1]
    lb = cs - first

    def log2_gate(x):
        return jnp.log2(lb + (1.0 - lb) * (1.0 / (1.0 + jnp.exp(-x))))

    _project_rows(a_ref, [wb_ref], o_ref, log2_gate)


def _mm_gate(a, w, layer, col0, n, lbraw, direction, *, bm=1024, bn=1024, name="mm_gate"):
    t, k = a.shape
    cb = col0 // bn
    nl = lbraw.shape[1]
    return pl.pallas_call(
        functools.partial(_gate_kernel, layer=layer),
        grid=(n // bn, t // bm),
        in_specs=[pl.BlockSpec((bm, k), lambda j, i: (i, 0)),
                  pl.BlockSpec((None, k, bn), lambda j, i: (layer, 0, cb + j)),
                  pl.BlockSpec((None, nl, bn), lambda j, i: (direction, 0, j))],
        out_specs=pl.BlockSpec((bm, bn), lambda j, i: (i, j)),
        out_shape=jax.ShapeDtypeStruct((t, n), jnp.float32),
        scratch_shapes=[pltpu.VMEM((k, bn), jnp.bfloat16)],
        compiler_params=_cparams("parallel", "arbitrary"),
        name=name,
    )(a, w, lbraw)


def _mm_res_kernel(a_ref, w_ref, x_ref, wpost_ref, wnext_ref, xo_ref, *h_ref):
    m = jnp.dot(a_ref[...], w_ref[...], preferred_element_type=jnp.float32)
    xn = x_ref[...] + _rms(m, wpost_ref[...])
    xo_ref[...] = xn
    if h_ref:
        h_ref[0][...] = _rms(xn, wnext_ref[...]).astype(h_ref[0].dtype)


def _mm_res(a, w, layer, x, wpost, wnext, *, bm, name="mm_res"):
    t, k = a.shape
    d = w.shape[2]
    emit_h = wnext is not None
    row = pl.BlockSpec((bm, d), lambda i: (i, 0))
    vec = pl.BlockSpec((1, d), lambda i: (0, 0))
    out_shape = [jax.ShapeDtypeStruct((t, d), jnp.float32)]
    if emit_h:
        out_shape.append(jax.ShapeDtypeStruct((t, d), jnp.bfloat16))
    else:
        wnext = wpost
    res = pl.pallas_call(
        _mm_res_kernel,
        grid=(t // bm,),
        in_specs=[pl.BlockSpec((bm, k), lambda i: (i, 0)),
                  pl.BlockSpec((None, k, d), lambda i: (layer, 0, 0), pipeline_mode=pl.Buffered(1)),
                  row, vec, vec],
        out_specs=[row] * len(out_shape),
        out_shape=out_shape,
        compiler_params=_cparams("parallel"),
        name=name,
    )(a, w, x, wpost.reshape(1, d), wnext.reshape(1, d))
    return (res[0], res[1]) if emit_h else (res[0], None)


def _conv_proj_kernel(ap_ref, a_ref, an_ref, *refs, mode, tiles_per_seq):
    nw, nc = (2, 2) if mode == "ffn" else (3, 1)
    w_refs, c_refs, o_ref = refs[:nw], refs[nw:nw + nc], refs[nw + nc]
    wb_refs, u_refs = refs[nw + nc + 1:2 * nw + nc + 1], refs[2 * nw + nc + 1:]
    for w_ref, wb_ref in zip(w_refs, wb_refs):
        _cast_weights(w_ref, wb_ref)
    bm = a_ref.shape[0]
    sb = CONV_SUB_ROWS
    nsub = bm // sb
    pos = pl.program_id(1) % tiles_per_seq
    keep_prev = (pos != 0).astype(jnp.float32)
    keep_next = (pos != tiles_per_seq - 1).astype(jnp.float32)
    splits = [0] + [HALO + r * sb for r in range(1, nsub)] + [bm + 2 * HALO]

    def project(r):
        rows = slice(splits[r], splits[r + 1])
        lhs = a_ref[r * sb:(r + 1) * sb, :]
        if r == 0:
            lhs = jnp.concatenate([ap_ref[...], lhs], axis=0)
        if r == nsub - 1:
            lhs = jnp.concatenate([lhs, an_ref[...]], axis=0)
        accs = [jnp.dot(lhs, wb_ref[...], preferred_element_type=jnp.float32) for wb_ref in wb_refs]
        vals = accs if mode == "ffn" else [accs[0], accs[1] * accs[2]]
        for u_ref, val in zip(u_refs, vals):
            u_ref[rows, :] = val
            if r == 0:
                u_ref[HALO - 1:HALO, :] = u_ref[HALO - 1:HALO, :] * keep_prev
            if r == nsub - 1:
                u_ref[HALO + bm:HALO + bm + 1, :] = u_ref[HALO + bm:HALO + bm + 1, :] * keep_next

    def conv(u_ref, c_ref, r0):
        g = V7X_SUBLANES
        ue = u_ref[r0 - g:r0 + sb + g, :]
        n = sb + 2 * g
        prev = pltpu.roll(ue, 1, 0)[g:g + sb]
        nxt = pltpu.roll(ue, n - 1, 0)[g:g + sb]
        return c_ref[0:1, :] * prev + c_ref[1:2, :] * ue[g:g + sb] + c_ref[2:3, :] * nxt

    def finish(r):
        r0 = HALO + r * sb
        if mode == "ffn":
            y = _silu(conv(u_refs[0], c_refs[0], r0)) * conv(u_refs[1], c_refs[1], r0)
        else:
            y = u_refs[0][r0:r0 + sb, :] * conv(u_refs[1], c_refs[0], r0)
        o_ref[r * sb:(r + 1) * sb, :] = y.astype(o_ref.dtype)

    project(0)
    for r in range(1, nsub):
        project(r)
        finish(r - 1)
    finish(nsub - 1)


def _conv_proj(a, w, w_conv, layer, n, seq, mode, *, bm=1024, bn=512, name="conv_proj"):
    t, k = a.shape
    nw, nc = (2, 2) if mode == "ffn" else (3, 1)
    nb = n // bn
    hb = bm // HALO
    last = t // HALO - 1

    def seg(s, rows):
        return pl.BlockSpec((None, rows, bn), lambda j, i: (layer, 0, s * nb + j))

    return pl.pallas_call(
        functools.partial(_conv_proj_kernel, mode=mode, tiles_per_seq=seq // bm),
        grid=(nb, t // bm),
        in_specs=[pl.BlockSpec((HALO, k), lambda j, i: (jnp.maximum(i * hb - 1, 0), 0)),
                  pl.BlockSpec((bm, k), lambda j, i: (i, 0)),
                  pl.BlockSpec((HALO, k), lambda j, i: (jnp.minimum((i + 1) * hb, last), 0))]
                 + [seg(s, k) for s in range(nw)] + [seg(s, 3) for s in range(nc)],
        out_specs=pl.BlockSpec((bm, bn), lambda j, i: (i, j)),
        out_shape=jax.ShapeDtypeStruct((t, n), jnp.bfloat16),
        scratch_shapes=[pltpu.VMEM((k, bn), jnp.bfloat16)] * nw
                       + [pltpu.VMEM((bm + 2 * HALO, bn), jnp.float32)] * 2,
        compiler_params=_cparams("parallel", "arbitrary"),
        name=name,
    )(a, a, a, *([w] * nw), *([w_conv] * nc))


def _cumsum_rows(x, reverse):
    n, cols = x.shape
    g = V7X_SUBLANES
    x = x.reshape(n // g, g, cols)
    row = lax.broadcasted_iota(jnp.int32, x.shape, 1)
    off = 1
    while off < g:
        if reverse:
            x = x + jnp.where(row < g - off, pltpu.roll(x, g - off, 1), 0.0)
        else:
            x = x + jnp.where(row >= off, pltpu.roll(x, off, 1), 0.0)
        off *= 2
    order = range(n // g - 1, -1, -1) if reverse else range(n // g)
    edge = 0 if reverse else g - 1
    out = [None] * (n // g)
    carry = None
    for i in order:
        xi = x[i] if carry is None else x[i] + carry
        out[i] = xi
        carry = xi[edge:edge + 1]
    return jnp.concatenate(out, axis=0)


def _cross_block_operands(q, k, b, reverse):
    c = CHUNK
    zero = jnp.zeros((BAND, HEAD_DIM), jnp.float32)
    levels = []
    s = c // 2
    while s >= BAND:
        nblk = c // (2 * s)
        q_rows, k_slabs, groups = [], [], []
        for blk in range(nblk):
            lo = blk * 2 * s
            m = lo + s if reverse else lo + s - 1
            bm = b[m:m + 1]
            k_rows = []
            for r0 in range(0, c, BAND):
                rows = slice(r0, r0 + BAND)
                inside = lo <= r0 < lo + 2 * s
                queries = inside and ((r0 >= lo + s) != reverse)
                k_rows.append(k[rows] * jnp.exp2(bm - b[rows]) if inside and not queries else zero)
                if queries:
                    qv = q[rows] * jnp.exp2(b[rows] - bm)
                    q_rows.append(jnp.concatenate([qv if j == blk else zero for j in range(nblk)], axis=1))
                    groups.append(r0 // BAND)
            k_slabs.append(jnp.concatenate(k_rows, axis=0))
        levels.append((jnp.concatenate(q_rows, axis=0).astype(jnp.bfloat16),
                       jnp.concatenate(k_slabs, axis=1).astype(jnp.bfloat16), groups))
        s //= 2
    return levels


def _chunk_operands(q, lf, wpad_ref, reverse):
    c = CHUNK
    b = _cumsum_rows(lf, reverse)
    f = jnp.exp2(lf)
    k = 1.0 - f
    b_last = b[0:1] if reverse else b[c - 1:c]
    levels = _cross_block_operands(q, k, b, reverse)
    w = k
    xs = [(q * w).astype(jnp.bfloat16)]
    for d in range(1, BAND):
        wpad_ref[d - 1, BAND:BAND + c, :] = w
        s0 = BAND + 1 if reverse else BAND - 1
        w = f * wpad_ref[d - 1, s0:s0 + c, :]
        xs.append((q * w).astype(jnp.bfloat16))
    return dict(
        qe=(q * jnp.exp2(b)).astype(jnp.bfloat16),
        kdec=(k * jnp.exp2(b_last - b)).astype(jnp.bfloat16),
        decay=jnp.exp2(b_last),
        levels=levels, x=jnp.concatenate(xs, axis=1))


def _hgrn_kernel(q_ref, lff_ref, lfb_ref, v_ref, gs_ref, gw_ref, wself_ref, wselb_ref, o_ref,
                 of_ref, ob_ref, sf_ref, sb_ref, wpf_ref, wpb_ref):
    l = q_ref.shape[1]
    nchunk = l // CHUNK
    sf_ref[...] = jnp.zeros_like(sf_ref)
    sb_ref[...] = jnp.zeros_like(sb_ref)
    for wp_ref in (wpf_ref, wpb_ref):
        margin = jnp.zeros(wp_ref.shape[:2] + (BAND, wp_ref.shape[3]), wp_ref.dtype)
        wp_ref[:, :, 0:BAND, :] = margin
        wp_ref[:, :, BAND + CHUNK:, :] = margin

    r_i = lax.broadcasted_iota(jnp.int32, (CHUNK, CHUNK), 0)
    c_i = lax.broadcasted_iota(jnp.int32, (CHUNK, CHUNK), 1)
    same = (r_i // BAND) == (c_i // BAND)
    mask_f = same & (c_i <= r_i)
    mask_b = same & (c_i >= r_i)
    nt = (((1,), (1,)), ((), ()))
    tn = (((0,), (0,)), ((), ()))
    f32 = jnp.float32

    def body(ci, carry):
        chains = []
        for u in range(UNROLL):
            cf = ci * UNROLL + u
            for reverse in (False, True):
                r0 = pl.multiple_of((nchunk - 1 - cf if reverse else cf) * CHUNK, CHUNK)
                rows = pl.ds(r0, CHUNK)
                lf_ref, wp_ref = (lfb_ref, wpb_ref) if reverse else (lff_ref, wpf_ref)
                ch = _chunk_operands(q_ref[0, rows, :].astype(f32), lf_ref[0, rows, :],
                                     wp_ref.at[u], reverse)
                ch.update(rows=rows, reverse=reverse, v=v_ref[0, rows, :])
                chains.append(ch)
        for reverse in (False, True):
            group = [ch for ch in chains if ch["reverse"] == reverse]
            wsel = wselb_ref[...] if reverse else wself_ref[...]
            bands = jnp.dot(jnp.concatenate([ch["x"] for ch in group], axis=0), wsel,
                            preferred_element_type=f32)
            for i, ch in enumerate(group):
                ch["bands"] = bands[i * CHUNK:(i + 1) * CHUNK]
        for ch in chains:
            pieces = [[] for _ in range(CHUNK // BAND)]
            for q_op, k_op, groups in ch["levels"]:
                s_l = lax.dot_general(q_op, k_op, nt, preferred_element_type=f32)
                for idx, grp in enumerate(groups):
                    pieces[grp].append(s_l[idx * BAND:(idx + 1) * BAND])
            ch["cross"] = jnp.concatenate(
                [sum(p[1:], p[0]) if p else jnp.zeros((BAND, CHUNK), f32) for p in pieces], axis=0)
            ch["kv"] = lax.dot_general(ch["v"], ch["kdec"], tn, preferred_element_type=f32)
        for ch in chains:
            bands = pltpu.roll(ch["bands"], 0, 1, stride=1, stride_axis=0)[:, :CHUNK]
            scores = jnp.where(mask_b if ch["reverse"] else mask_f, bands, ch["cross"])
            ch["intra"] = jnp.dot(scores.astype(jnp.bfloat16), ch["v"], preferred_element_type=f32)
        state = {False: sf_ref[...], True: sb_ref[...]}
        for ch in chains:
            st = state[ch["reverse"]]
            inter = lax.dot_general(ch["qe"], st.astype(jnp.bfloat16), nt, preferred_element_type=f32)
            (ob_ref if ch["reverse"] else of_ref)[ch["rows"], :] = inter + ch["intra"]
            state[ch["reverse"]] = st * ch["decay"] + ch["kv"]
        sf_ref[...] = state[False]
        sb_ref[...] = state[True]
        return carry

    lax.fori_loop(0, nchunk // UNROLL, body, 0)

    def fin(ci, carry):
        rows = pl.ds(pl.multiple_of(ci * FIN_ROWS, FIN_ROWS), FIN_ROWS)
        o = of_ref[rows, :] + ob_ref[rows, :]
        y = _rms(o, gw_ref[...]) * gs_ref[0, rows, :].astype(jnp.float32)
        o_ref[0, rows, :] = y.astype(o_ref.dtype)
        return carry

    lax.fori_loop(0, l // FIN_ROWS, fin, 0)


def _band_selectors():
    d = jnp.arange(BAND * HEAD_DIM, dtype=jnp.int32) // HEAD_DIM
    lane = jnp.arange(V7X_LANES, dtype=jnp.int32)
    fwd = (lane[None, :] == ((V7X_LANES - d) % V7X_LANES)[:, None]).astype(jnp.bfloat16)
    bwd = (lane[None, :] == d[:, None]).astype(jnp.bfloat16)
    return fwd, bwd


def _hgrn_recurrence(q, lff, lfb, v, gs, gw):
    b, l, d = q.shape
    hd = HEAD_DIM
    blk = pl.BlockSpec((1, l, hd), lambda i, j: (i, 0, j))
    sel = pl.BlockSpec((BAND * hd, V7X_LANES), lambda i, j: (0, 0))
    f32 = jnp.float32
    wsel_f, wsel_b = _band_selectors()
    return pl.pallas_call(
        _hgrn_kernel,
        grid=(b, d // hd),
        in_specs=[blk, blk, blk, blk, blk, pl.BlockSpec((1, hd), lambda i, j: (0, 0)), sel, sel],
        out_specs=blk,
        out_shape=jax.ShapeDtypeStruct((b, l, d), jnp.bfloat16),
        scratch_shapes=[pltpu.VMEM((l, hd), f32), pltpu.VMEM((l, hd), f32),
                        pltpu.VMEM((hd, hd), f32), pltpu.VMEM((hd, hd), f32),
                        pltpu.VMEM((UNROLL, BAND - 1, CHUNK + 2 * BAND, hd), f32),
                        pltpu.VMEM((UNROLL, BAND - 1, CHUNK + 2 * BAND, hd), f32)],
        compiler_params=_cparams("parallel", "parallel"),
        name="hgrn_recurrence",
    )(q, lff, lfb, v, gs, gw.reshape(1, hd), wsel_f, wsel_b)


def kernel(x, hgrn_w_in, hgrn_w_out, hgrn_gnorm, hgrn_lower_bounds, sconv_w_in, sconv_w_conv, sconv_w_out,
           ffn_w_in, ffn_w_conv, ffn_w_out, norm_pre_mix, norm_post_mix, norm_pre_ffn, norm_post_ffn):
    bsz, seq, d = x.shape
    t = bsz * seq
    depth = norm_pre_mix.shape[0]
    ffn_dim = ffn_w_out.shape[1]
    bf16 = jnp.bfloat16
    sh = (bsz, seq, d)

    xf = x.reshape(t, d)
    h = _norm(xf, norm_pre_mix[0])
    hgrn_w_out_b, sconv_w_out_b, ffn_w_out_b = (w.astype(bf16) for w in (hgrn_w_out, sconv_w_out, ffn_w_out))
    for i in range(depth):
        j = i // 2
        if i % 2 == 0:
            q = _mm(h, hgrn_w_in, j, 0, d, act="silu", name="hgrn_q")
            lff = _mm_gate(h, hgrn_w_in, j, d, d, hgrn_lower_bounds, 0, name="hgrn_gate_f")
            lfb = _mm_gate(h, hgrn_w_in, j, 2 * d, d, hgrn_lower_bounds, 1, name="hgrn_gate_b")
            v = _mm(h, hgrn_w_in, j, 3 * d, d, name="hgrn_v")
            gs = _mm(h, hgrn_w_in, j, 4 * d, d, act="silu", name="hgrn_g")
            o = _hgrn_recurrence(q.reshape(sh), lff.reshape(sh), lfb.reshape(sh), v.reshape(sh),
                                 gs.reshape(sh), hgrn_gnorm[j])
            a, w_out = o.reshape(t, d), hgrn_w_out_b
        else:
            a = _conv_proj(h, sconv_w_in, sconv_w_conv, j, d, seq, "sconv", name="sconv_in")
            w_out = sconv_w_out_b
        xf, h = _mm_res(a, w_out, j, xf, norm_post_mix[i], norm_pre_ffn[i], bm=512, name="mix_out")

        a = _conv_proj(h, ffn_w_in, ffn_w_conv, i, ffn_dim, seq, "ffn", name="ffn_in")
        w_next = norm_pre_mix[i + 1] if i + 1 < depth else None
        xf, h = _mm_res(a, ffn_w_out_b, i, xf, norm_post_ffn[i], w_next, bm=256, name="ffn_out")
    return xf.reshape(bsz, seq, d)
```

```python
import functools

import jax
import jax.numpy as jnp
from jax import lax
from jax.experimental import pallas as pl
from jax.experimental.pallas import tpu as pltpu

HEAD_DIM = 128
EPS = 1e-6

V7X_LANES = 128
V7X_SUBLANES = 8
V7X_VMEM_LIMIT_BYTES = 56 * 1024 * 1024

CHUNK = 64
BAND = 8
UNROLL = 16
FIN_ROWS = 512
HALO = 16
MM_SUB = 4
CONV_SUB_ROWS = 512


def _cparams(*sem):
    return pltpu.CompilerParams(dimension_semantics=sem, vmem_limit_bytes=V7X_VMEM_LIMIT_BYTES)


def _silu(x):
    return x * (1.0 / (1.0 + jnp.exp(-x)))


def _rms(x, w):
    return x * lax.rsqrt(jnp.mean(x * x, axis=-1, keepdims=True) + EPS) * w


def _norm_kernel(x_ref, w_ref, h_ref):
    h_ref[...] = _rms(x_ref[...], w_ref[...]).astype(h_ref.dtype)


def _norm(x, w, bm=512):
    t, d = x.shape
    return pl.pallas_call(
        _norm_kernel,
        grid=(t // bm,),
        in_specs=[pl.BlockSpec((bm, d), lambda i: (i, 0)), pl.BlockSpec((1, d), lambda i: (0, 0))],
        out_specs=pl.BlockSpec((bm, d), lambda i: (i, 0)),
        out_shape=jax.ShapeDtypeStruct((t, d), jnp.bfloat16),
        compiler_params=_cparams("parallel"),
        name="rmsnorm",
    )(x, w.reshape(1, d))


def _cast_weights(w_ref, wb_ref):
    @pl.when(pl.program_id(1) == 0)
    def _():
        wb_ref[...] = w_ref[...].astype(wb_ref.dtype)


def _project_rows(a_ref, wb_refs, o_ref, epilogue):
    sb = a_ref.shape[0] // MM_SUB
    pending = None
    for r in range(MM_SUB):
        rows = slice(r * sb, (r + 1) * sb)
        accs = [jnp.dot(a_ref[rows, :], wb[...], preferred_element_type=jnp.float32) for wb in wb_refs]
        if pending is not None:
            o_ref[pending[0], :] = epilogue(*pending[1]).astype(o_ref.dtype)
        pending = (rows, accs)
    o_ref[pending[0], :] = epilogue(*pending[1]).astype(o_ref.dtype)


def _mm_kernel(a_ref, w_ref, o_ref, wb_ref, *, act):
    _cast_weights(w_ref, wb_ref)
    _project_rows(a_ref, [wb_ref], o_ref, _silu if act == "silu" else (lambda acc: acc))


def _mm(a, w, layer, col0, n, *, act=None, bm=1024, bn=1024, name="mm"):
    t, k = a.shape
    cb = col0 // bn
    return pl.pallas_call(
        functools.partial(_mm_kernel, act=act),
        grid=(n // bn, t // bm),
        in_specs=[pl.BlockSpec((bm, k), lambda j, i: (i, 0)),
                  pl.BlockSpec((None, k, bn), lambda j, i: (layer, 0, cb + j))],
        out_specs=pl.BlockSpec((bm, bn), lambda j, i: (i, j)),
        out_shape=jax.ShapeDtypeStruct((t, n), jnp.bfloat16),
        scratch_shapes=[pltpu.VMEM((k, bn), jnp.bfloat16)],
        compiler_params=_cparams("parallel", "arbitrary"),
        name=name,
    )(a, w)


def _gate_kernel(a_ref, w_ref, lbraw_ref, o_ref, wb_ref, *, layer):
    _cast_weights(w_ref, wb_ref)
    raw = lbraw_ref[...]
    e = jnp.exp(raw - jnp.max(raw, axis=0, keepdims=True))
    p = e / jnp.sum(e, axis=0, keepdims=True)
    cs = p[0:1]
    first = cs
    for r in range(1, layer + 1):
        cs = cs + p[r:r + 1]
    lb = cs - first

    def log2_gate(x):
        return jnp.log2(lb + (1.0 - lb) * (1.0 / (1.0 + jnp.exp(-x))))

    _project_rows(a_ref, [wb_ref], o_ref, log2_gate)


def _mm_gate(a, w, layer, col0, n, lbraw, direction, *, bm=1024, bn=1024, name="mm_gate"):
    t, k = a.shape
    cb = col0 // bn
    nl = lbraw.shape[1]
    return pl.pallas_call(
        functools.partial(_gate_kernel, layer=layer),
        grid=(n // bn, t // bm),
        in_specs=[pl.BlockSpec((bm, k), lambda j, i: (i, 0)),
                  pl.BlockSpec((None, k, bn), lambda j, i: (layer, 0, cb + j)),
                  pl.BlockSpec((None, nl, bn), lambda j, i: (direction, 0, j))],
        out_specs=pl.BlockSpec((bm, bn), lambda j, i: (i, j)),
        out_shape=jax.ShapeDtypeStruct((t, n), jnp.float32),
        scratch_shapes=[pltpu.VMEM((k, bn), jnp.bfloat16)],
        compiler_params=_cparams("parallel", "arbitrary"),
        name=name,
    )(a, w, lbraw)


def _mm_res_kernel(a_ref, w_ref, x_ref, wpost_ref, wnext_ref, xo_ref, *h_ref):
    m = jnp.dot(a_ref[...], w_ref[...], preferred_element_type=jnp.float32)
    xn = x_ref[...] + _rms(m, wpost_ref[...])
    xo_ref[...] = xn
    if h_ref:
        h_ref[0][...] = _rms(xn, wnext_ref[...]).astype(h_ref[0].dtype)


def _mm_res(a, w, layer, x, wpost, wnext, *, bm, name="mm_res"):
    t, k = a.shape
    d = w.shape[2]
    emit_h = wnext is not None
    row = pl.BlockSpec((bm, d), lambda i: (i, 0))
    vec = pl.BlockSpec((1, d), lambda i: (0, 0))
    out_shape = [jax.ShapeDtypeStruct((t, d), jnp.float32)]
    if emit_h:
        out_shape.append(jax.ShapeDtypeStruct((t, d), jnp.bfloat16))
    else:
        wnext = wpost
    res = pl.pallas_call(
        _mm_res_kernel,
        grid=(t // bm,),
        in_specs=[pl.BlockSpec((bm, k), lambda i: (i, 0)),
                  pl.BlockSpec((None, k, d), lambda i: (layer, 0, 0), pipeline_mode=pl.Buffered(1)),
                  row, vec, vec],
        out_specs=[row] * len(out_shape),
        out_shape=out_shape,
        compiler_params=_cparams("parallel"),
        name=name,
    )(a, w, x, wpost.reshape(1, d), wnext.reshape(1, d))
    return (res[0], res[1]) if emit_h else (res[0], None)


def _conv_proj_kernel(ap_ref, a_ref, an_ref, *refs, mode, tiles_per_seq):
    nw, nc = (2, 2) if mode == "ffn" else (3, 1)
    w_refs, c_refs, o_ref = refs[:nw], refs[nw:nw + nc], refs[nw + nc]
    wb_refs, u_refs = refs[nw + nc + 1:2 * nw + nc + 1], refs[2 * nw + nc + 1:]
    for w_ref, wb_ref in zip(w_refs, wb_refs):
        _cast_weights(w_ref, wb_ref)
    bm = a_ref.shape[0]
    sb = CONV_SUB_ROWS
    nsub = bm // sb
    pos = pl.program_id(1) % tiles_per_seq
    keep_prev = (pos != 0).astype(jnp.float32)
    keep_next = (pos != tiles_per_seq - 1).astype(jnp.float32)
    splits = [0] + [HALO + r * sb for r in range(1, nsub)] + [bm + 2 * HALO]

    def project(r):
        rows = slice(splits[r], splits[r + 1])
        lhs = a_ref[r * sb:(r + 1) * sb, :]
        if r == 0:
            lhs = jnp.concatenate([ap_ref[...], lhs], axis=0)
        if r == nsub - 1:
            lhs = jnp.concatenate([lhs, an_ref[...]], axis=0)
        accs = [jnp.dot(lhs, wb_ref[...], preferred_element_type=jnp.float32) for wb_ref in wb_refs]
        vals = accs if mode == "ffn" else [accs[0], accs[1] * accs[2]]
        for u_ref, val in zip(u_refs, vals):
            u_ref[rows, :] = val
            if r == 0:
                u_ref[HALO - 1:HALO, :] = u_ref[HALO - 1:HALO, :] * keep_prev
            if r == nsub - 1:
                u_ref[HALO + bm:HALO + bm + 1, :] = u_ref[HALO + bm:HALO + bm + 1, :] * keep_next

    def conv(u_ref, c_ref, r0):
        g = V7X_SUBLANES
        ue = u_ref[r0 - g:r0 + sb + g, :]
        n = sb + 2 * g
        prev = pltpu.roll(ue, 1, 0)[g:g + sb]
        nxt = pltpu.roll(ue, n - 1, 0)[g:g + sb]
        return c_ref[0:1, :] * prev + c_ref[1:2, :] * ue[g:g + sb] + c_ref[2:3, :] * nxt

    def finish(r):
        r0 = HALO + r * sb
        if mode == "ffn":
            y = _silu(conv(u_refs[0], c_refs[0], r0)) * conv(u_refs[1], c_refs[1], r0)
        else:
            y = u_refs[0][r0:r0 + sb, :] * conv(u_refs[1], c_refs[0], r0)
        o_ref[r * sb:(r + 1) * sb, :] = y.astype(o_ref.dtype)

    project(0)
    for r in range(1, nsub):
        project(r)
        finish(r - 1)
    finish(nsub - 1)


def _conv_proj(a, w, w_conv, layer, n, seq, mode, *, bm=1024, bn=512, name="conv_proj"):
    t, k = a.shape
    nw, nc = (2, 2) if mode == "ffn" else (3, 1)
    nb = n // bn
    hb = bm // HALO
    last = t // HALO - 1

    def seg(s, rows):
        return pl.BlockSpec((None, rows, bn), lambda j, i: (layer, 0, s * nb + j))

    return pl.pallas_call(
        functools.partial(_conv_proj_kernel, mode=mode, tiles_per_seq=seq // bm),
        grid=(nb, t // bm),
        in_specs=[pl.BlockSpec((HALO, k), lambda j, i: (jnp.maximum(i * hb - 1, 0), 0)),
                  pl.BlockSpec((bm, k), lambda j, i: (i, 0)),
                  pl.BlockSpec((HALO, k), lambda j, i: (jnp.minimum((i + 1) * hb, last), 0))]
                 + [seg(s, k) for s in range(nw)] + [seg(s, 3) for s in range(nc)],
        out_specs=pl.BlockSpec((bm, bn), lambda j, i: (i, j)),
        out_shape=jax.ShapeDtypeStruct((t, n), jnp.bfloat16),
        scratch_shapes=[pltpu.VMEM((k, bn), jnp.bfloat16)] * nw
                       + [pltpu.VMEM((bm + 2 * HALO, bn), jnp.float32)] * 2,
        compiler_params=_cparams("parallel", "arbitrary"),
        name=name,
    )(a, a, a, *([w] * nw), *([w_conv] * nc))


def _cumsum_rows(x, reverse):
    n, cols = x.shape
    g = V7X_SUBLANES
    x = x.reshape(n // g, g, cols)
    row = lax.broadcasted_iota(jnp.int32, x.shape, 1)
    off = 1
    while off < g:
        if reverse:
            x = x + jnp.where(row < g - off, pltpu.roll(x, g - off, 1), 0.0)
        else:
            x = x + jnp.where(row >= off, pltpu.roll(x, off, 1), 0.0)
        off *= 2
    order = range(n // g - 1, -1, -1) if reverse else range(n // g)
    edge = 0 if reverse else g - 1
    out = [None] * (n // g)
    carry = None
    for i in order:
        xi = x[i] if carry is None else x[i] + carry
        out[i] = xi
        carry = xi[edge:edge + 1]
    return jnp.concatenate(out, axis=0)


def _cross_block_operands(q, k, b, reverse):
    c = CHUNK
    zero = jnp.zeros((BAND, HEAD_DIM), jnp.float32)
    levels = []
    s = c // 2
    while s >= BAND:
        nblk = c // (2 * s)
        q_rows, k_slabs, groups = [], [], []
        for blk in range(nblk):
            lo = blk * 2 * s
            m = lo + s if reverse else lo + s - 1
            bm = b[m:m + 1]
            k_rows = []
            for r0 in range(0, c, BAND):
                rows = slice(r0, r0 + BAND)
                inside = lo <= r0 < lo + 2 * s
                queries = inside and ((r0 >= lo + s) != reverse)
                k_rows.append(k[rows] * jnp.exp2(bm - b[rows]) if inside and not queries else zero)
                if queries:
                    qv = q[rows] * jnp.exp2(b[rows] - bm)
                    q_rows.append(jnp.concatenate([qv if j == blk else zero for j in range(nblk)], axis=1))
                    groups.append(r0 // BAND)
            k_slabs.append(jnp.concatenate(k_rows, axis=0))
        levels.append((jnp.concatenate(q_rows, axis=0).astype(jnp.bfloat16),
                       jnp.concatenate(k_slabs, axis=1).astype(jnp.bfloat16), groups))
        s //= 2
    return levels


def _chunk_operands(q, lf, wpad_ref, reverse):
    c = CHUNK
    b = _cumsum_rows(lf, reverse)
    f = jnp.exp2(lf)
    k = 1.0 - f
    b_last = b[0:1] if reverse else b[c - 1:c]
    levels = _cross_block_operands(q, k, b, reverse)
    w = k
    xs = [(q * w).astype(jnp.bfloat16)]
    for d in range(1, BAND):
        wpad_ref[d - 1, BAND:BAND + c, :] = w
        s0 = BAND + 1 if reverse else BAND - 1
        w = f * wpad_ref[d - 1, s0:s0 + c, :]
        xs.append((q * w).astype(jnp.bfloat16))
    return dict(
        qe=(q * jnp.exp2(b)).astype(jnp.bfloat16),
        kdec=(k * jnp.exp2(b_last - b)).astype(jnp.bfloat16),
        decay=jnp.exp2(b_last),
        levels=levels, x=jnp.concatenate(xs, axis=1))


def _hgrn_kernel(q_ref, lff_ref, lfb_ref, v_ref, gs_ref, gw_ref, wself_ref, wselb_ref, o_ref,
                 of_ref, ob_ref, sf_ref, sb_ref, wpf_ref, wpb_ref):
    l = q_ref.shape[1]
    nchunk = l // CHUNK
    sf_ref[...] = jnp.zeros_like(sf_ref)
    sb_ref[...] = jnp.zeros_like(sb_ref)
    for wp_ref in (wpf_ref, wpb_ref):
        margin = jnp.zeros(wp_ref.shape[:2] + (BAND, wp_ref.shape[3]), wp_ref.dtype)
        wp_ref[:, :, 0:BAND, :] = margin
        wp_ref[:, :, BAND + CHUNK:, :] = margin

    r_i = lax.broadcasted_iota(jnp.int32, (CHUNK, CHUNK), 0)
    c_i = lax.broadcasted_iota(jnp.int32, (CHUNK, CHUNK), 1)
    same = (r_i // BAND) == (c_i // BAND)
    mask_f = same & (c_i <= r_i)
    mask_b = same & (c_i >= r_i)
    nt = (((1,), (1,)), ((), ()))
    tn = (((0,), (0,)), ((), ()))
    f32 = jnp.float32

    def body(ci, carry):
        chains = []
        for u in range(UNROLL):
            cf = ci * UNROLL + u
            for reverse in (False, True):
                r0 = pl.multiple_of((nchunk - 1 - cf if reverse else cf) * CHUNK, CHUNK)
                rows = pl.ds(r0, CHUNK)
                lf_ref, wp_ref = (lfb_ref, wpb_ref) if reverse else (lff_ref, wpf_ref)
                ch = _chunk_operands(q_ref[0, rows, :].astype(f32), lf_ref[0, rows, :],
                                     wp_ref.at[u], reverse)
                ch.update(rows=rows, reverse=reverse, v=v_ref[0, rows, :])
                chains.append(ch)
        for reverse in (False, True):
            group = [ch for ch in chains if ch["reverse"] == reverse]
            wsel = wselb_ref[...] if reverse else wself_ref[...]
            bands = jnp.dot(jnp.concatenate([ch["x"] for ch in group], axis=0), wsel,
                            preferred_element_type=f32)
            for i, ch in enumerate(group):
                ch["bands"] = bands[i * CHUNK:(i + 1) * CHUNK]
        for ch in chains:
            pieces = [[] for _ in range(CHUNK // BAND)]
            for q_op, k_op, groups in ch["levels"]:
                s_l = lax.dot_general(q_op, k_op, nt, preferred_element_type=f32)
                for idx, grp in enumerate(groups):
                    pieces[grp].append(s_l[idx * BAND:(idx + 1) * BAND])
            ch["cross"] = jnp.concatenate(
                [sum(p[1:], p[0]) if p else jnp.zeros((BAND, CHUNK), f32) for p in pieces], axis=0)
            ch["kv"] = lax.dot_general(ch["v"], ch["kdec"], tn, preferred_element_type=f32)
        for ch in chains:
            bands = pltpu.roll(ch["bands"], 0, 1, stride=1, stride_axis=0)[:, :CHUNK]
            scores = jnp.where(mask_b if ch["reverse"] else mask_f, bands, ch["cross"])
            ch["intra"] = jnp.dot(scores.astype(jnp.bfloat16), ch["v"], preferred_element_type=f32)
        state = {False: sf_ref[...], True: sb_ref[...]}
        for ch in chains:
            st = state[ch["reverse"]]
            inter = lax.dot_general(ch["qe"], st.astype(jnp.bfloat16), nt, preferred_element_type=f32)
            (ob_ref if ch["reverse"] else of_ref)[ch["rows"], :] = inter + ch["intra"]
            state[ch["reverse"]] = st * ch["decay"] + ch["kv"]
        sf_ref[...] = state[False]
        sb_ref[...] = state[True]
        return carry

    lax.fori_loop(0, nchunk // UNROLL, body, 0)

    def fin(ci, carry):
        rows = pl.ds(pl.multiple_of(ci * FIN_ROWS, FIN_ROWS), FIN_ROWS)
        o = of_ref[rows, :] + ob_ref[rows, :]
        y = _rms(o, gw_ref[...]) * gs_ref[0, rows, :].astype(jnp.float32)
        o_ref[0, rows, :] = y.astype(o_ref.dtype)
        return carry

    lax.fori_loop(0, l // FIN_ROWS, fin, 0)


def _band_selectors():
    d = jnp.arange(BAND * HEAD_DIM, dtype=jnp.int32) // HEAD_DIM
    lane = jnp.arange(V7X_LANES, dtype=jnp.int32)
    fwd = (lane[None, :] == ((V7X_LANES - d) % V7X_LANES)[:, None]).astype(jnp.bfloat16)
    bwd = (lane[None, :] == d[:, None]).astype(jnp.bfloat16)
    return fwd, bwd


def _hgrn_recurrence(q, lff, lfb, v, gs, gw):
    b, l, d = q.shape
    hd = HEAD_DIM
    blk = pl.BlockSpec((1, l, hd), lambda i, j: (i, 0, j))
    sel = pl.BlockSpec((BAND * hd, V7X_LANES), lambda i, j: (0, 0))
    f32 = jnp.float32
    wsel_f, wsel_b = _band_selectors()
    return pl.pallas_call(
        _hgrn_kernel,
        grid=(b, d // hd),
        in_specs=[blk, blk, blk, blk, blk, pl.BlockSpec((1, hd), lambda i, j: (0, 0)), sel, sel],
        out_specs=blk,
        out_shape=jax.ShapeDtypeStruct((b, l, d), jnp.bfloat16),
        scratch_shapes=[pltpu.VMEM((l, hd), f32), pltpu.VMEM((l, hd), f32),
                        pltpu.VMEM((hd, hd), f32), pltpu.VMEM((hd, hd), f32),
                        pltpu.VMEM((UNROLL, BAND - 1, CHUNK + 2 * BAND, hd), f32),
                        pltpu.VMEM((UNROLL, BAND - 1, CHUNK + 2 * BAND, hd), f32)],
        compiler_params=_cparams("parallel", "parallel"),
        name="hgrn_recurrence",
    )(q, lff, lfb, v, gs, gw.reshape(1, hd), wsel_f, wsel_b)


def kernel(x, hgrn_w_in, hgrn_w_out, hgrn_gnorm, hgrn_lower_bounds, sconv_w_in, sconv_w_conv, sconv_w_out,
           ffn_w_in, ffn_w_conv, ffn_w_out, norm_pre_mix, norm_post_mix, norm_pre_ffn, norm_post_ffn):
    bsz, seq, d = x.shape
    t = bsz * seq
    depth = norm_pre_mix.shape[0]
    ffn_dim = ffn_w_out.shape[1]
    bf16 = jnp.bfloat16
    sh = (bsz, seq, d)

    xf = x.reshape(t, d)
    h = _norm(xf, norm_pre_mix[0])
    hgrn_w_out_b, sconv_w_out_b, ffn_w_out_b = (w.astype(bf16) for w in (hgrn_w_out, sconv_w_out, ffn_w_out))
    for i in range(depth):
        j = i // 2
        if i % 2 == 0:
            q = _mm(h, hgrn_w_in, j, 0, d, act="silu", name="hgrn_q")
            lff = _mm_gate(h, hgrn_w_in, j, d, d, hgrn_lower_bounds, 0, name="hgrn_gate_f")
            lfb = _mm_gate(h, hgrn_w_in, j, 2 * d, d, hgrn_lower_bounds, 1, name="hgrn_gate_b")
            v = _mm(h, hgrn_w_in, j, 3 * d, d, name="hgrn_v")
            gs = _mm(h, hgrn_w_in, j, 4 * d, d, act="silu", name="hgrn_g")
            o = _hgrn_recurrence(q.reshape(sh), lff.reshape(sh), lfb.reshape(sh), v.reshape(sh),
                                 gs.reshape(sh), hgrn_gnorm[j])
            a, w_out = o.reshape(t, d), hgrn_w_out_b
        else:
            a = _conv_proj(h, sconv_w_in, sconv_w_conv, j, d, seq, "sconv", name="sconv_in")
            w_out = sconv_w_out_b
        xf, h = _mm_res(a, w_out, j, xf, norm_post_mix[i], norm_pre_ffn[i], bm=512, name="mix_out")

        a = _conv_proj(h, ffn_w_in, ffn_w_conv, i, ffn_dim, seq, "ffn", name="ffn_in")
        w_next = norm_pre_mix[i + 1] if i + 1 < depth else None
        xf, h = _mm_res(a, ffn_w_out_b, i, xf, norm_post_ffn[i], w_next, bm=256, name="ffn_out")
    return xf.reshape(bsz, seq, d)
```

```python
import functools

import jax
import jax.numpy as jnp
from jax import lax
from jax.experimental import pallas as pl
from jax.experimental.pallas import tpu as pltpu

HEAD_DIM = 128
EPS = 1e-6

V7X_LANES = 128
V7X_SUBLANES = 8
V7X_VMEM_LIMIT_BYTES = 56 * 1024 * 1024

CHUNK = 64
BAND = 8
UNROLL = 16
FIN_ROWS = 512
HALO = 16
MM_SUB = 4
CONV_SUB_ROWS = 1024


def _cparams(*sem):
    return pltpu.CompilerParams(dimension_semantics=sem, vmem_limit_bytes=V7X_VMEM_LIMIT_BYTES)


def _silu(x):
    return x * (1.0 / (1.0 + jnp.exp(-x)))


def _rms(x, w):
    return x * lax.rsqrt(jnp.mean(x * x, axis=-1, keepdims=True) + EPS) * w


def _norm_kernel(x_ref, w_ref, h_ref):
    h_ref[...] = _rms(x_ref[...], w_ref[...]).astype(h_ref.dtype)


def _norm(x, w, bm=512):
    t, d = x.shape
    return pl.pallas_call(
        _norm_kernel,
        grid=(t // bm,),
        in_specs=[pl.BlockSpec((bm, d), lambda i: (i, 0)), pl.BlockSpec((1, d), lambda i: (0, 0))],
        out_specs=pl.BlockSpec((bm, d), lambda i: (i, 0)),
        out_shape=jax.ShapeDtypeStruct((t, d), jnp.bfloat16),
        compiler_params=_cparams("parallel"),
        name="rmsnorm",
    )(x, w.reshape(1, d))


def _cast_weights(w_ref, wb_ref):
    @pl.when(pl.program_id(1) == 0)
    def _():
        wb_ref[...] = w_ref[...].astype(wb_ref.dtype)


def _project_rows(a_ref, wb_refs, o_ref, epilogue):
    sb = a_ref.shape[0] // MM_SUB
    pending = None
    for r in range(MM_SUB):
        rows = slice(r * sb, (r + 1) * sb)
        accs = [jnp.dot(a_ref[rows, :], wb[...], preferred_element_type=jnp.float32) for wb in wb_refs]
        if pending is not None:
            o_ref[pending[0], :] = epilogue(*pending[1]).astype(o_ref.dtype)
        pending = (rows, accs)
    o_ref[pending[0], :] = epilogue(*pending[1]).astype(o_ref.dtype)


def _mm_kernel(a_ref, w_ref, o_ref, wb_ref, *, act):
    _cast_weights(w_ref, wb_ref)
    _project_rows(a_ref, [wb_ref], o_ref, _silu if act == "silu" else (lambda acc: acc))


def _mm(a, w, layer, col0, n, *, act=None, bm=1024, bn=1024, name="mm"):
    t, k = a.shape
    cb = col0 // bn
    return pl.pallas_call(
        functools.partial(_mm_kernel, act=act),
        grid=(n // bn, t // bm),
        in_specs=[pl.BlockSpec((bm, k), lambda j, i: (i, 0)),
                  pl.BlockSpec((None, k, bn), lambda j, i: (layer, 0, cb + j))],
        out_specs=pl.BlockSpec((bm, bn), lambda j, i: (i, j)),
        out_shape=jax.ShapeDtypeStruct((t, n), jnp.bfloat16),
        scratch_shapes=[pltpu.VMEM((k, bn), jnp.bfloat16)],
        compiler_params=_cparams("parallel", "arbitrary"),
        name=name,
    )(a, w)


def _gate_kernel(a_ref, w_ref, lbraw_ref, o_ref, wb_ref, *, layer):
    _cast_weights(w_ref, wb_ref)
    raw = lbraw_ref[...]
    e = jnp.exp(raw - jnp.max(raw, axis=0, keepdims=True))
    p = e / jnp.sum(e, axis=0, keepdims=True)
    cs = p[0:1]
    first = cs
    for r in range(1, layer + 1):
        cs = cs + p[r:r + 1]
    lb = cs - first

    def log2_gate(x):
        return jnp.log2(lb + (1.0 - lb) * (1.0 / (1.0 + jnp.exp(-x))))

    _project_rows(a_ref, [wb_ref], o_ref, log2_gate)


def _mm_gate(a, w, layer, col0, n, lbraw, direction, *, bm=1024, bn=1024, name="mm_gate"):
    t, k = a.shape
    cb = col0 // bn
    nl = lbraw.shape[1]
    return pl.pallas_call(
        functools.partial(_gate_kernel, layer=layer),
        grid=(n // bn, t // bm),
        in_specs=[pl.BlockSpec((bm, k), lambda j, i: (i, 0)),
                  pl.BlockSpec((None, k, bn), lambda j, i: (layer, 0, cb + j)),
                  pl.BlockSpec((None, nl, bn), lambda j, i: (direction, 0, j))],
        out_specs=pl.BlockSpec((bm, bn), lambda j, i: (i, j)),
        out_shape=jax.ShapeDtypeStruct((t, n), jnp.float32),
        scratch_shapes=[pltpu.VMEM((k, bn), jnp.bfloat16)],
        compiler_params=_cparams("parallel", "arbitrary"),
        name=name,
    )(a, w, lbraw)


def _mm_res_kernel(a_ref, w_ref, x_ref, wpost_ref, wnext_ref, xo_ref, *h_ref):
    m = jnp.dot(a_ref[...], w_ref[...], preferred_element_type=jnp.float32)
    xn = x_ref[...] + _rms(m, wpost_ref[...])
    xo_ref[...] = xn
    if h_ref:
        h_ref[0][...] = _rms(xn, wnext_ref[...]).astype(h_ref[0].dtype)


def _mm_res(a, w, layer, x, wpost, wnext, *, bm, name="mm_res"):
    t, k = a.shape
    d = w.shape[2]
    emit_h = wnext is not None
    row = pl.BlockSpec((bm, d), lambda i: (i, 0))
    vec = pl.BlockSpec((1, d), lambda i: (0, 0))
    out_shape = [jax.ShapeDtypeStruct((t, d), jnp.float32)]
    if emit_h:
        out_shape.append(jax.ShapeDtypeStruct((t, d), jnp.bfloat16))
    else:
        wnext = wpost
    res = pl.pallas_call(
        _mm_res_kernel,
        grid=(t // bm,),
        in_specs=[pl.BlockSpec((bm, k), lambda i: (i, 0)),
                  pl.BlockSpec((None, k, d), lambda i: (layer, 0, 0), pipeline_mode=pl.Buffered(1)),
                  row, vec, vec],
        out_specs=[row] * len(out_shape),
        out_shape=out_shape,
        compiler_params=_cparams("parallel"),
        name=name,
    )(a, w, x, wpost.reshape(1, d), wnext.reshape(1, d))
    return (res[0], res[1]) if emit_h else (res[0], None)


def _conv_proj_kernel(ap_ref, a_ref, an_ref, *refs, mode, tiles_per_seq):
    nw, nc = (2, 2) if mode == "ffn" else (3, 1)
    w_refs, c_refs, o_ref = refs[:nw], refs[nw:nw + nc], refs[nw + nc]
    wb_refs, u_refs = refs[nw + nc + 1:2 * nw + nc + 1], refs[2 * nw + nc + 1:]
    for w_ref, wb_ref in zip(w_refs, wb_refs):
        _cast_weights(w_ref, wb_ref)
    bm = a_ref.shape[0]
    sb = CONV_SUB_ROWS
    nsub = bm // sb
    pos = pl.program_id(1) % tiles_per_seq
    keep_prev = (pos != 0).astype(jnp.float32)
    keep_next = (pos != tiles_per_seq - 1).astype(jnp.float32)
    splits = [0] + [HALO + r * sb for r in range(1, nsub)] + [bm + 2 * HALO]

    def project(r):
        rows = slice(splits[r], splits[r + 1])
        lhs = a_ref[r * sb:(r + 1) * sb, :]
        if r == 0:
            lhs = jnp.concatenate([ap_ref[...], lhs], axis=0)
        if r == nsub - 1:
            lhs = jnp.concatenate([lhs, an_ref[...]], axis=0)
        accs = [jnp.dot(lhs, wb_ref[...], preferred_element_type=jnp.float32) for wb_ref in wb_refs]
        vals = accs if mode == "ffn" else [accs[0], accs[1] * accs[2]]
        for u_ref, val in zip(u_refs, vals):
            u_ref[rows, :] = val
            if r == 0:
                u_ref[HALO - 1:HALO, :] = u_ref[HALO - 1:HALO, :] * keep_prev
            if r == nsub - 1:
                u_ref[HALO + bm:HALO + bm + 1, :] = u_ref[HALO + bm:HALO + bm + 1, :] * keep_next

    def conv(u_ref, c_ref, r0):
        g = V7X_SUBLANES
        ue = u_ref[r0 - g:r0 + sb + g, :]
        n = sb + 2 * g
        prev = pltpu.roll(ue, 1, 0)[g:g + sb]
        nxt = pltpu.roll(ue, n - 1, 0)[g:g + sb]
        return c_ref[0:1, :] * prev + c_ref[1:2, :] * ue[g:g + sb] + c_ref[2:3, :] * nxt

    def finish(r):
        r0 = HALO + r * sb
        if mode == "ffn":
            y = _silu(conv(u_refs[0], c_refs[0], r0)) * conv(u_refs[1], c_refs[1], r0)
        else:
            y = u_refs[0][r0:r0 + sb, :] * conv(u_refs[1], c_refs[0], r0)
        o_ref[r * sb:(r + 1) * sb, :] = y.astype(o_ref.dtype)

    project(0)
    for r in range(1, nsub):
        project(r)
        finish(r - 1)
    finish(nsub - 1)


def _conv_proj(a, w, w_conv, layer, n, seq, mode, *, bm=1024, bn=512, name="conv_proj"):
    t, k = a.shape
    nw, nc = (2, 2) if mode == "ffn" else (3, 1)
    nb = n // bn
    hb = bm // HALO
    last = t // HALO - 1

    def seg(s, rows):
        return pl.BlockSpec((None, rows, bn), lambda j, i: (layer, 0, s * nb + j))

    return pl.pallas_call(
        functools.partial(_conv_proj_kernel, mode=mode, tiles_per_seq=seq // bm),
        grid=(nb, t // bm),
        in_specs=[pl.BlockSpec((HALO, k), lambda j, i: (jnp.maximum(i * hb - 1, 0), 0)),
                  pl.BlockSpec((bm, k), lambda j, i: (i, 0)),
                  pl.BlockSpec((HALO, k), lambda j, i: (jnp.minimum((i + 1) * hb, last), 0))]
                 + [seg(s, k) for s in range(nw)] + [seg(s, 3) for s in range(nc)],
        out_specs=pl.BlockSpec((bm, bn), lambda j, i: (i, j)),
        out_shape=jax.ShapeDtypeStruct((t, n), jnp.bfloat16),
        scratch_shapes=[pltpu.VMEM((k, bn), jnp.bfloat16)] * nw
                       + [pltpu.VMEM((bm + 2 * HALO, bn), jnp.float32)] * 2,
        compiler_params=_cparams("parallel", "arbitrary"),
        name=name,
    )(a, a, a, *([w] * nw), *([w_conv] * nc))


def _cumsum_rows(x, reverse):
    n, cols = x.shape
    g = V7X_SUBLANES
    x = x.reshape(n // g, g, cols)
    row = lax.broadcasted_iota(jnp.int32, x.shape, 1)
    off = 1
    while off < g:
        if reverse:
            x = x + jnp.where(row < g - off, pltpu.roll(x, g - off, 1), 0.0)
        else:
            x = x + jnp.where(row >= off, pltpu.roll(x, off, 1), 0.0)
        off *= 2
    order = range(n // g - 1, -1, -1) if reverse else range(n // g)
    edge = 0 if reverse else g - 1
    out = [None] * (n // g)
    carry = None
    for i in order:
        xi = x[i] if carry is None else x[i] + carry
        out[i] = xi
        carry = xi[edge:edge + 1]
    return jnp.concatenate(out, axis=0)


def _cross_block_operands(q, k, b, reverse):
    c = CHUNK
    zero = jnp.zeros((BAND, HEAD_DIM), jnp.float32)
    levels = []
    s = c // 2
    while s >= BAND:
        nblk = c // (2 * s)
        q_rows, k_slabs, groups = [], [], []
        for blk in range(nblk):
            lo = blk * 2 * s
            m = lo + s if reverse else lo + s - 1
            bm = b[m:m + 1]
            k_rows = []
            for r0 in range(0, c, BAND):
                rows = slice(r0, r0 + BAND)
                inside = lo <= r0 < lo + 2 * s
                queries = inside and ((r0 >= lo + s) != reverse)
                k_rows.append(k[rows] * jnp.exp2(bm - b[rows]) if inside and not queries else zero)
                if queries:
                    qv = q[rows] * jnp.exp2(b[rows] - bm)
                    q_rows.append(jnp.concatenate([qv if j == blk else zero for j in range(nblk)], axis=1))
                    groups.append(r0 // BAND)
            k_slabs.append(jnp.concatenate(k_rows, axis=0))
        levels.append((jnp.concatenate(q_rows, axis=0).astype(jnp.bfloat16),
                       jnp.concatenate(k_slabs, axis=1).astype(jnp.bfloat16), groups))
        s //= 2
    return levels


def _chunk_operands(q, lf, wpad_ref, reverse):
    c = CHUNK
    b = _cumsum_rows(lf, reverse)
    f = jnp.exp2(lf)
    k = 1.0 - f
    b_last = b[0:1] if reverse else b[c - 1:c]
    levels = _cross_block_operands(q, k, b, reverse)
    w = k
    xs = [(q * w).astype(jnp.bfloat16)]
    for d in range(1, BAND):
        wpad_ref[d - 1, BAND:BAND + c, :] = w
        s0 = BAND + 1 if reverse else BAND - 1
        w = f * wpad_ref[d - 1, s0:s0 + c, :]
        xs.append((q * w).astype(jnp.bfloat16))
    return dict(
        qe=(q * jnp.exp2(b)).astype(jnp.bfloat16),
        kdec=(k * jnp.exp2(b_last - b)).astype(jnp.bfloat16),
        decay=jnp.exp2(b_last),
        levels=levels, x=jnp.concatenate(xs, axis=1))


def _hgrn_kernel(q_ref, lff_ref, lfb_ref, v_ref, gs_ref, gw_ref, wself_ref, wselb_ref, o_ref,
                 of_ref, ob_ref, sf_ref, sb_ref, wpf_ref, wpb_ref):
    l = q_ref.shape[1]
    nchunk = l // CHUNK
    sf_ref[...] = jnp.zeros_like(sf_ref)
    sb_ref[...] = jnp.zeros_like(sb_ref)
    for wp_ref in (wpf_ref, wpb_ref):
        margin = jnp.zeros(wp_ref.shape[:2] + (BAND, wp_ref.shape[3]), wp_ref.dtype)
        wp_ref[:, :, 0:BAND, :] = margin
        wp_ref[:, :, BAND + CHUNK:, :] = margin

    r_i = lax.broadcasted_iota(jnp.int32, (CHUNK, CHUNK), 0)
    c_i = lax.broadcasted_iota(jnp.int32, (CHUNK, CHUNK), 1)
    same = (r_i // BAND) == (c_i // BAND)
    mask_f = same & (c_i <= r_i)
    mask_b = same & (c_i >= r_i)
    nt = (((1,), (1,)), ((), ()))
    tn = (((0,), (0,)), ((), ()))
    f32 = jnp.float32

    def body(ci, carry):
        chains = []
        for u in range(UNROLL):
            cf = ci * UNROLL + u
            for reverse in (False, True):
                r0 = pl.multiple_of((nchunk - 1 - cf if reverse else cf) * CHUNK, CHUNK)
                rows = pl.ds(r0, CHUNK)
                lf_ref, wp_ref = (lfb_ref, wpb_ref) if reverse else (lff_ref, wpf_ref)
                ch = _chunk_operands(q_ref[0, rows, :].astype(f32), lf_ref[0, rows, :],
                                     wp_ref.at[u], reverse)
                ch.update(rows=rows, reverse=reverse, v=v_ref[0, rows, :])
                chains.append(ch)
        for reverse in (False, True):
            group = [ch for ch in chains if ch["reverse"] == reverse]
            wsel = wselb_ref[...] if reverse else wself_ref[...]
            bands = jnp.dot(jnp.concatenate([ch["x"] for ch in group], axis=0), wsel,
                            preferred_element_type=f32)
            for i, ch in enumerate(group):
                ch["bands"] = bands[i * CHUNK:(i + 1) * CHUNK]
        for ch in chains:
            pieces = [[] for _ in range(CHUNK // BAND)]
            for q_op, k_op, groups in ch["levels"]:
                s_l = lax.dot_general(q_op, k_op, nt, preferred_element_type=f32)
                for idx, grp in enumerate(groups):
                    pieces[grp].append(s_l[idx * BAND:(idx + 1) * BAND])
            ch["cross"] = jnp.concatenate(
                [sum(p[1:], p[0]) if p else jnp.zeros((BAND, CHUNK), f32) for p in pieces], axis=0)
            ch["kv"] = lax.dot_general(ch["v"], ch["kdec"], tn, preferred_element_type=f32)
        for ch in chains:
            bands = pltpu.roll(ch["bands"], 0, 1, stride=1, stride_axis=0)[:, :CHUNK]
            scores = jnp.where(mask_b if ch["reverse"] else mask_f, bands, ch["cross"])
            ch["intra"] = jnp.dot(scores.astype(jnp.bfloat16), ch["v"], preferred_element_type=f32)
        state = {False: sf_ref[...], True: sb_ref[...]}
        for ch in chains:
            st = state[ch["reverse"]]
            inter = lax.dot_general(ch["qe"], st.astype(jnp.bfloat16), nt, preferred_element_type=f32)
            (ob_ref if ch["reverse"] else of_ref)[ch["rows"], :] = inter + ch["intra"]
            state[ch["reverse"]] = st * ch["decay"] + ch["kv"]
        sf_ref[...] = state[False]
        sb_ref[...] = state[True]
        return carry

    lax.fori_loop(0, nchunk // UNROLL, body, 0)

    def fin(ci, carry):
        rows = pl.ds(pl.multiple_of(ci * FIN_ROWS, FIN_ROWS), FIN_ROWS)
        o = of_ref[rows, :] + ob_ref[rows, :]
        y = _rms(o, gw_ref[...]) * gs_ref[0, rows, :].astype(jnp.float32)
        o_ref[0, rows, :] = y.astype(o_ref.dtype)
        return carry

    lax.fori_loop(0, l // FIN_ROWS, fin, 0)


def _band_selectors():
    d = jnp.arange(BAND * HEAD_DIM, dtype=jnp.int32) // HEAD_DIM
    lane = jnp.arange(V7X_LANES, dtype=jnp.int32)
    fwd = (lane[None, :] == ((V7X_LANES - d) % V7X_LANES)[:, None]).astype(jnp.bfloat16)
    bwd = (lane[None, :] == d[:, None]).astype(jnp.bfloat16)
    return fwd, bwd


def _hgrn_recurrence(q, lff, lfb, v, gs, gw):
    b, l, d = q.shape
    hd = HEAD_DIM
    blk = pl.BlockSpec((1, l, hd), lambda i, j: (i, 0, j))
    sel = pl.BlockSpec((BAND * hd, V7X_LANES), lambda i, j: (0, 0))
    f32 = jnp.float32
    wsel_f, wsel_b = _band_selectors()
    return pl.pallas_call(
        _hgrn_kernel,
        grid=(b, d // hd),
        in_specs=[blk, blk, blk, blk, blk, pl.BlockSpec((1, hd), lambda i, j: (0, 0)), sel, sel],
        out_specs=blk,
        out_shape=jax.ShapeDtypeStruct((b, l, d), jnp.bfloat16),
        scratch_shapes=[pltpu.VMEM((l, hd), f32), pltpu.VMEM((l, hd), f32),
                        pltpu.VMEM((hd, hd), f32), pltpu.VMEM((hd, hd), f32),
                        pltpu.VMEM((UNROLL, BAND - 1, CHUNK + 2 * BAND, hd), f32),
                        pltpu.VMEM((UNROLL, BAND - 1, CHUNK + 2 * BAND, hd), f32)],
        compiler_params=_cparams("parallel", "parallel"),
        name="hgrn_recurrence",
    )(q, lff, lfb, v, gs, gw.reshape(1, hd), wsel_f, wsel_b)


def kernel(x, hgrn_w_in, hgrn_w_out, hgrn_gnorm, hgrn_lower_bounds, sconv_w_in, sconv_w_conv, sconv_w_out,
           ffn_w_in, ffn_w_conv, ffn_w_out, norm_pre_mix, norm_post_mix, norm_pre_ffn, norm_post_ffn):
    bsz, seq, d = x.shape
    t = bsz * seq
    depth = norm_pre_mix.shape[0]
    ffn_dim = ffn_w_out.shape[1]
    bf16 = jnp.bfloat16
    sh = (bsz, seq, d)

    xf = x.reshape(t, d)
    h = _norm(xf, norm_pre_mix[0])
    hgrn_w_out_b, sconv_w_out_b, ffn_w_out_b = (w.astype(bf16) for w in (hgrn_w_out, sconv_w_out, ffn_w_out))
    for i in range(depth):
        j = i // 2
        if i % 2 == 0:
            q = _mm(h, hgrn_w_in, j, 0, d, act="silu", name="hgrn_q")
            lff = _mm_gate(h, hgrn_w_in, j, d, d, hgrn_lower_bounds, 0, name="hgrn_gate_f")
            lfb = _mm_gate(h, hgrn_w_in, j, 2 * d, d, hgrn_lower_bounds, 1, name="hgrn_gate_b")
            v = _mm(h, hgrn_w_in, j, 3 * d, d, name="hgrn_v")
            gs = _mm(h, hgrn_w_in, j, 4 * d, d, act="silu", name="hgrn_g")
            o = _hgrn_recurrence(q.reshape(sh), lff.reshape(sh), lfb.reshape(sh), v.reshape(sh),
                                 gs.reshape(sh), hgrn_gnorm[j])
            a, w_out = o.reshape(t, d), hgrn_w_out_b
        else:
            a = _conv_proj(h, sconv_w_in, sconv_w_conv, j, d, seq, "sconv", name="sconv_in")
            w_out = sconv_w_out_b
        xf, h = _mm_res(a, w_out, j, xf, norm_post_mix[i], norm_pre_ffn[i], bm=512, name="mix_out")

        a = _conv_proj(h, ffn_w_in, ffn_w_conv, i, ffn_dim, seq, "ffn", name="ffn_in")
        w_next = norm_pre_mix[i + 1] if i + 1 < depth else None
        xf, h = _mm_res(a, ffn_w_out_b, i, xf, norm_post_ffn[i], w_next, bm=256, name="ffn_out")
    return xf.reshape(bsz, seq, d)
```
